```python
import math
import jax, jax.numpy as jnp
from jax import lax
import numpy as np

D_MODEL = 2048
BATCH = 32
SEQ = 256
DEPTH = 2
DEC_BATCH = 8
DEC_SEQ = 2048
PAST_LEN = 512

GRID_W = 64
N_HEADS_A = 8
DH_A = 64
DV_A = 2 * DH_A
WIDTH_A = N_HEADS_A * DV_A
QK_A = 2 * N_HEADS_A * DH_A
N_HEADS_R = 4
DK_R = 64
DV_R = 128
WIDTH_R = N_HEADS_R * DV_R
QK_R = N_HEADS_R * DK_R
RET_CHUNK = 128
POOL_WINDOWS = (2, 4, 8, 16)
N_POOL_GROUPS = 4
POOL_GROUP_DIM = 128
WIDTH_P = N_POOL_GROUPS * POOL_GROUP_DIM
MIX_WIDTH = WIDTH_A + WIDTH_R + WIDTH_P
IN_SPLITS = (QK_A, QK_A, WIDTH_A, QK_R, QK_R, WIDTH_R, WIDTH_R, WIDTH_R, WIDTH_P)
IN_WIDTH = sum(IN_SPLITS)
D_FF = 5632
ROPE_THETA = 10000.0
EPS = 1e-6
Q_BLOCK = 128

kernel_name = 'hybrid_diffattn_retention_pool_diffusion_step'

F32 = jnp.float32


def rmsnorm(x, g):
    xf = x.astype(F32)
    y = xf * lax.rsqrt(jnp.mean(xf * xf, -1, keepdims=True) + EPS)
    return (y * g.astype(F32)).astype(x.dtype)


def head_rmsnorm(x):
    xf = x.astype(F32)
    return (xf * lax.rsqrt(jnp.mean(xf * xf, -1, keepdims=True) + EPS)).astype(x.dtype)


def split_columns(p):
    idx = np.cumsum(IN_SPLITS)[:-1].tolist()
    return jnp.split(p, idx, axis=-1)


def axial_rope_tables(n_tokens):
    rows = n_tokens // GRID_W
    row = jnp.repeat(jnp.arange(rows, dtype=F32), GRID_W)
    col = jnp.tile(jnp.arange(GRID_W, dtype=F32), rows)
    n_freq = DH_A // 4
    inv = ROPE_THETA ** (-jnp.arange(n_freq, dtype=F32) / n_freq)
    ang_r = (row[:, None] * inv)[:, None, :]
    ang_c = (col[:, None] * inv)[:, None, :]
    return (jnp.cos(ang_r), jnp.sin(ang_r), jnp.cos(ang_c), jnp.sin(ang_c))


def rotate(x, cos, sin):
    x1, x2 = jnp.split(x, 2, -1)
    return jnp.concatenate([x1 * cos - x2 * sin, x2 * cos + x1 * sin], -1)


def axial_rope(x, cos_r, sin_r, cos_c, sin_c):
    xr, xc = jnp.split(x.astype(F32), 2, -1)
    return jnp.concatenate([rotate(xr, cos_r, sin_r), rotate(xc, cos_c, sin_c)], -1).astype(x.dtype)


def diff_lambda(lp, layer_idx):
    lam_init = 0.8 - 0.6 * math.exp(-0.3 * layer_idx)
    lpf = lp.astype(F32)
    lam = jnp.exp(jnp.sum(lpf[0] * lpf[1])) - jnp.exp(jnp.sum(lpf[2] * lpf[3])) + lam_init
    return lam, lam_init


def diff_attention(q, k, v, lam):
    b, nq = q.shape[0], q.shape[1]
    nb = nq // Q_BLOCK
    qb = q.reshape(b, nb, Q_BLOCK, 2 * N_HEADS_A, DH_A).transpose(1, 0, 2, 3, 4)
    scale = DH_A ** -0.5

    def one_block(q_blk):
        s = jnp.einsum('bqhd,bkhd->bhqk', q_blk, k).astype(F32) * scale
        p = jax.nn.softmax(s, axis=-1).reshape(b, N_HEADS_A, 2, Q_BLOCK, -1)
        w = p[:, :, 0] - lam * p[:, :, 1]
        return jnp.einsum('bhqk,bkhd->bqhd', w.astype(v.dtype), v)

    out = lax.map(one_block, qb)
    return out.transpose(1, 0, 2, 3, 4).reshape(b, nq, N_HEADS_A, DV_A)


def retention_chunked(q, k, v, log_gamma, s0):
    b, h, n, _ = q.shape
    nc, c = n // RET_CHUNK, RET_CHUNK
    qc = q.reshape(b, h, nc, c, DK_R)
    kc = k.reshape(b, h, nc, c, DK_R)
    vc = v.reshape(b, h, nc, c, DV_R)
    lg = log_gamma.astype(F32)[:, None]
    idx = jnp.arange(c, dtype=F32)
    diff = idx[:, None] - idx[None, :]
    decay_mask = jnp.where(diff >= 0, jnp.exp(lg[..., None] * jnp.maximum(diff, 0.0)), 0.0)
    scores = jnp.einsum('bhnid,bhnjd->bhnij', qc, kc).astype(F32) * decay_mask[None, :, None]
    o_inner = jnp.einsum('bhnij,bhnjv->bhniv', scores, vc.astype(F32))
    k_decay = jnp.exp(lg * (c - 1.0 - idx))
    kv = jnp.einsum('bhnjd,hj,bhnjv->bhndv', kc.astype(F32), k_decay, vc.astype(F32))
    chunk_decay = jnp.exp(lg[:, 0] * c)[None, :, None, None]

    def step(s, kv_n):
        return s * chunk_decay + kv_n, s

    s_final, s_prev = lax.scan(step, s0.astype(F32), kv.transpose(2, 0, 1, 3, 4))
    s_prev = s_prev.transpose(1, 2, 0, 3, 4)
    q_decay = jnp.exp(lg * (idx + 1.0))
    o_cross = jnp.einsum('bhnid,hi,bhndv->bhniv', qc.astype(F32), q_decay, s_prev)
    o = (o_inner + o_cross).reshape(b, h, n, DV_R)
    return o.astype(v.dtype), s_final.astype(v.dtype)


def bidir_retention(qr, kr, vr, gf, gb, log_decay, s0):
    b, n, _ = qr.shape
    q = qr.reshape(b, n, N_HEADS_R, DK_R).transpose(0, 2, 1, 3)
    k = kr.reshape(b, n, N_HEADS_R, DK_R).transpose(0, 2, 1, 3) * (DK_R ** -0.5)
    v = vr.reshape(b, n, N_HEADS_R, DV_R).transpose(0, 2, 1, 3)
    o_f, s_f = retention_chunked(q, k, v, log_decay[0], s0[:, 0])
    o_b, s_b = retention_chunked(jnp.flip(q, 2), jnp.flip(k, 2), jnp.flip(v, 2), log_decay[1], s0[:, 1])
    o_b = jnp.flip(o_b, 2)
    y_f = head_rmsnorm(o_f.transpose(0, 2, 1, 3))
    y_b = head_rmsnorm(o_b.transpose(0, 2, 1, 3))
    y = (jax.nn.silu(gf.reshape(b, n, N_HEADS_R, DV_R)) * y_f
         + jax.nn.silu(gb.reshape(b, n, N_HEADS_R, DV_R)) * y_b)
    return y.reshape(b, n, WIDTH_R), jnp.stack([s_f, s_b], axis=1)


def pool_mixer(xp, w, scale):
    b, n, _ = xp.shape
    xf = xp.astype(F32)
    cs = jnp.pad(jnp.cumsum(xf, axis=1), ((0, 0), (1, 0), (0, 0)))
    t = jnp.arange(n)
    means = []
    for g, win in enumerate(POOL_WINDOWS):
        lo = jnp.clip(t - win // 2, 0, n)
        hi = jnp.clip(t - win // 2 + win, 0, n)
        cg = cs[..., g * POOL_GROUP_DIM:(g + 1) * POOL_GROUP_DIM]
        means.append((cg[:, hi] - cg[:, lo]) / (hi - lo).astype(F32)[None, :, None])
    pooled = jnp.stack(means, axis=2) - xf.reshape(b, n, N_POOL_GROUPS, POOL_GROUP_DIM)
    y = jnp.einsum('bngc,gcd->bngd', pooled.astype(xp.dtype), w).reshape(b, n, WIDTH_P)
    return y * scale


def conv_glu(h, w_up, conv_w, conv_b, w_down):
    gate, val = jnp.split(h @ w_up, 2, axis=-1)
    gp = jnp.pad(gate, ((0, 0), (1, 1), (0, 0)))
    gate = gp[:, :-2] * conv_w[0] + gp[:, 1:-1] * conv_w[1] + gp[:, 2:] * conv_w[2] + conv_b
    return (jax.nn.silu(gate) * val) @ w_down


def trunk_layer(x, mod, layer_idx, rope, ctx_k, ctx_v, ret_s0,
                norm1_g, w_in, attn_lambda, attn_subln_g, ret_log_decay, pool_w, pool_scale,
                w_out, norm2_g, ffn_w_up, ffn_conv_w, ffn_conv_b, ffn_w_down):
    b, n, _ = x.shape
    shift1, scale1, gate1, shift2, scale2, gate2 = jnp.split(mod, 6, axis=-1)
    h = rmsnorm(x, norm1_g) * (1 + scale1) + shift1
    qa, ka, va, qr, kr, vr, gf, gb, xpool = split_columns(h @ w_in)
    qa = qa.reshape(b, n, 2 * N_HEADS_A, DH_A)
    ka = ka.reshape(b, n, 2 * N_HEADS_A, DH_A)
    va = va.reshape(b, n, N_HEADS_A, DV_A)
    if rope is None:
        k_all, v_all = ka, va
        ret_s0 = jnp.zeros((b, 2, N_HEADS_R, DK_R, DV_R), x.dtype)
    else:
        qa = axial_rope(qa, *rope)
        k_all = jnp.concatenate([ctx_k.astype(x.dtype), axial_rope(ka, *rope)], axis=1)
        v_all = jnp.concatenate([ctx_v.astype(x.dtype), va], axis=1)
    lam, lam_init = diff_lambda(attn_lambda, layer_idx)
    oa = diff_attention(qa, k_all, v_all, lam)
    oa = (head_rmsnorm(oa) * attn_subln_g * (1.0 - lam_init)).reshape(b, n, WIDTH_A)
    orr, s_final = bidir_retention(qr, kr, vr, gf, gb, ret_log_decay, ret_s0)
    op = pool_mixer(xpool, pool_w, pool_scale)
    mixed = jnp.concatenate([oa, orr, op], axis=-1)
    x = x + gate1 * (mixed @ w_out)
    h = rmsnorm(x, norm2_g) * (1 + scale2) + shift2
    x = x + gate2 * conv_glu(h, ffn_w_up, ffn_conv_w, ffn_conv_b, ffn_w_down)
    return x, ka, va, s_final


def setup_inputs(seed: int = 0) -> dict:
    key = jax.random.key(seed)
    ks = jax.random.split(key, 24)
    D = D_MODEL

    def nrm(k, shape, s=1.0):
        return jax.random.normal(k, shape, F32) * s

    base_decay = jnp.log1p(-(2.0 ** (-5.0 - jnp.arange(N_HEADS_R, dtype=F32))))
    return {
        'x_prompt': nrm(ks[0], (BATCH, SEQ, D)),
        'x_sample': nrm(ks[1], (DEC_BATCH, DEC_SEQ, D)),
        'cache_k': nrm(ks[2], (DEC_BATCH, DEPTH, PAST_LEN, 2 * N_HEADS_A, DH_A)),
        'cache_v': nrm(ks[3], (DEC_BATCH, DEPTH, PAST_LEN, N_HEADS_A, DV_A)),
        'state_ret': nrm(ks[4], (DEC_BATCH, DEPTH, 2, N_HEADS_R, DK_R, DV_R), 0.5),
        'c': nrm(ks[5], (DEC_BATCH, D)),
        'c_ctx': nrm(ks[6], (D,)),
        'ada_w': nrm(ks[7], (DEPTH, D, 6 * D), 0.5 * D ** -0.5),
        'ada_b': nrm(ks[8], (DEPTH, 6 * D), 0.02),
        'norm1_g': 1.0 + nrm(ks[9], (DEPTH, D), 0.02),
        'w_in': nrm(ks[10], (DEPTH, D, IN_WIDTH), D ** -0.5),
        'attn_lambda': nrm(ks[11], (DEPTH, 4, DH_A), 0.1),
        'attn_subln_g': 1.0 + nrm(ks[12], (DEPTH, DV_A), 0.02),
        'ret_log_decay': base_decay * (1.0 + nrm(ks[13], (DEPTH, 2, N_HEADS_R), 0.05)),
        'pool_w': nrm(ks[14], (DEPTH, N_POOL_GROUPS, POOL_GROUP_DIM, POOL_GROUP_DIM), POOL_GROUP_DIM ** -0.5),
        'pool_scale': 1.0 + nrm(ks[15], (DEPTH, WIDTH_P), 0.02),
        'w_out': nrm(ks[16], (DEPTH, MIX_WIDTH, D), MIX_WIDTH ** -0.5),
        'norm2_g': 1.0 + nrm(ks[17], (DEPTH, D), 0.02),
        'ffn_w_up': nrm(ks[18], (DEPTH, D, 2 * D_FF), D ** -0.5),
        'ffn_conv_w': nrm(ks[19], (DEPTH, 3, D_FF), 3 ** -0.5),
        'ffn_conv_b': nrm(ks[20], (DEPTH, D_FF), 0.02),
        'ffn_w_down': nrm(ks[21], (DEPTH, D_FF, D), D_FF ** -0.5),
        'final_g': 1.0 + nrm(ks[22], (D,), 0.02),
    }


def reference(x_prompt, x_sample, cache_k, cache_v, state_ret, c, c_ctx, ada_w, ada_b, norm1_g, w_in,
              attn_lambda, attn_subln_g, ret_log_decay, pool_w, pool_scale, w_out, norm2_g,
              ffn_w_up, ffn_conv_w, ffn_conv_b, ffn_w_down, final_g):
    rope = axial_rope_tables(x_sample.shape[1])
    xp, xs = x_prompt, x_sample
    new_k, new_v, new_s = [], [], []
    for l in range(DEPTH):
        lw = (norm1_g[l], w_in[l], attn_lambda[l], attn_subln_g[l], ret_log_decay[l], pool_w[l],
              pool_scale[l], w_out[l], norm2_g[l], ffn_w_up[l], ffn_conv_w[l], ffn_conv_b[l], ffn_w_down[l])
        mod_ctx = (jax.nn.silu(c_ctx) @ ada_w[l] + ada_b[l])[None, None, :]
        xp, k_l, v_l, s_l = trunk_layer(xp, mod_ctx, l, None, None, None, None, *lw)
        new_k.append(k_l)
        new_v.append(v_l)
        new_s.append(s_l)
        mod_lat = (jax.nn.silu(c) @ ada_w[l] + ada_b[l])[:, None, :]
        xs, _, _, _ = trunk_layer(xs, mod_lat, l, rope, cache_k[:, l], cache_v[:, l], state_ret[:, l], *lw)
    y_prompt = rmsnorm(xp, final_g)
    y_sample = rmsnorm(xs, final_g)
    k_out = jnp.stack(new_k, axis=1)
    v_out = jnp.stack(new_v, axis=1)
    s_out = jnp.stack(new_s, axis=1)
    return (y_prompt, y_sample, k_out, v_out, s_out)
```

```python
import functools
import math

import jax
import jax.numpy as jnp
from jax import lax
from jax.experimental import pallas as pl
from jax.experimental.pallas import tpu as pltpu

F32 = jnp.float32
BF16 = jnp.bfloat16

D_MODEL = 2048
DEPTH = 2
GRID_W = 64
N_HEADS_A = 8
DH_A = 64
DV_A = 128
WIDTH_A = 1024
QK_A = 1024
N_HEADS_R = 4
DK_R = 64
DV_R = 128
WIDTH_R = 512
QK_R = 256
POOL_WINDOWS = (2, 4, 8, 16)
POOL_GROUP_DIM = 128
WIDTH_P = 512
MIX_WIDTH = 2048
IN_WIDTH = 5632
D_FF = 5632
ROPE_THETA = 10000.0
EPS = 1e-6

OFF_QA, OFF_KA, OFF_VA = 0, 1024, 2048
OFF_QR, OFF_KR, OFF_VR = 3072, 3328, 3584
OFF_GF, OFF_GB, OFF_POOL = 4096, 4608, 5120

LANES = 128
VMEM_LIMIT = 52 * 1024 * 1024
RET_CHUNK = 128


def _cp(*sem):
    return pltpu.CompilerParams(dimension_semantics=sem, vmem_limit_bytes=VMEM_LIMIT)


def _silu(x):
    return x * (1.0 / (1.0 + jnp.exp(-x)))


def _dot_nt(a, b):
    return lax.dot_general(a, b, (((1,), (1,)), ((), ())), preferred_element_type=F32)


def _dot(a, b):
    return jnp.dot(a, b, preferred_element_type=F32)


def _mod_kernel(c_ref, w_ref, b_ref, o_ref):
    a = _silu(c_ref[...]).astype(BF16)
    o_ref[0] = _dot(a, w_ref[0].astype(BF16)) + b_ref[0]


def _modulation(cs, ada_w, ada_b):
    tn = 1024
    n = ada_w.shape[-1]
    return pl.pallas_call(
        _mod_kernel,
        grid=(DEPTH, n // tn),
        in_specs=[
            pl.BlockSpec((16, D_MODEL), lambda l, j: (0, 0)),
            pl.BlockSpec((1, D_MODEL, tn), lambda l, j: (l, 0, j)),
            pl.BlockSpec((1, 1, tn), lambda l, j: (l, 0, j)),
        ],
        out_specs=pl.BlockSpec((1, 16, tn), lambda l, j: (l, 0, j)),
        out_shape=jax.ShapeDtypeStruct((DEPTH, 16, n), F32),
        compiler_params=_cp("arbitrary", "arbitrary"),
        name="modulation",
    )(cs, ada_w, ada_b.reshape(DEPTH, 1, n))


def _norm_mod_kernel(x_ref, g_ref, mod_ref, o_ref, *, shift_idx):
    x = x_ref[...]
    y = x * lax.rsqrt(jnp.mean(x * x, axis=-1, keepdims=True) + EPS)
    y = y * g_ref[...]
    shift = mod_ref[0, shift_idx:shift_idx + 1, :]
    scale = mod_ref[0, shift_idx + 1:shift_idx + 2, :]
    o_ref[...] = (y * (1.0 + scale) + shift).astype(o_ref.dtype)


def _norm_mod(x, g, mod, rows_per_mod, shift_idx):
    t = x.shape[0]
    tm = 512
    per = rows_per_mod // tm
    return pl.pallas_call(
        functools.partial(_norm_mod_kernel, shift_idx=shift_idx),
        grid=(t // tm,),
        in_specs=[
            pl.BlockSpec((tm, D_MODEL), lambda i: (i, 0)),
            pl.BlockSpec((1, D_MODEL), lambda i: (0, 0)),
            pl.BlockSpec((1, 6, D_MODEL), lambda i: (i // per, 0, 0)),
        ],
        out_specs=pl.BlockSpec((tm, D_MODEL), lambda i: (i, 0)),
        out_shape=jax.ShapeDtypeStruct((t, D_MODEL), BF16),
        compiler_params=_cp("arbitrary"),
        name="norm_mod",
    )(x, g.reshape(1, D_MODEL), mod)


def _final_norm_kernel(x_ref, g_ref, o_ref):
    x = x_ref[...]
    y = x * lax.rsqrt(jnp.mean(x * x, axis=-1, keepdims=True) + EPS)
    o_ref[...] = y * g_ref[...]


def _final_norm(x, g):
    t = x.shape[0]
    tm = 512
    return pl.pallas_call(
        _final_norm_kernel,
        grid=(t // tm,),
        in_specs=[
            pl.BlockSpec((tm, D_MODEL), lambda i: (i, 0)),
            pl.BlockSpec((1, D_MODEL), lambda i: (0, 0)),
        ],
        out_specs=pl.BlockSpec((tm, D_MODEL), lambda i: (i, 0)),
        out_shape=jax.ShapeDtypeStruct((t, D_MODEL), F32),
        compiler_params=_cp("arbitrary"),
        name="final_norm",
    )(x, g.reshape(1, D_MODEL))


def _mm_kernel(a_ref, w_ref, o_ref):
    o_ref[...] = _dot(a_ref[...], w_ref[...])


def _matmul(a, w, tm, tn):
    t, k = a.shape
    n = w.shape[1]
    return pl.pallas_call(
        _mm_kernel,
        grid=(t // tm, n // tn),
        in_specs=[
            pl.BlockSpec((tm, k), lambda i, j: (i, 0)),
            pl.BlockSpec((k, tn), lambda i, j: (0, j)),
        ],
        out_specs=pl.BlockSpec((tm, tn), lambda i, j: (i, j)),
        out_shape=jax.ShapeDtypeStruct((t, n), F32),
        compiler_params=_cp("arbitrary", "arbitrary"),
        name="matmul",
    )(a, w)


def _mm_res_kernel(a_ref, w_ref, x_ref, mod_ref, o_ref, *, gate_idx):
    acc = _dot(a_ref[...], w_ref[...])
    gate = mod_ref[0, gate_idx:gate_idx + 1, :]
    o_ref[...] = x_ref[...] + gate * acc


def _matmul_residual(a, w, x, mod, rows_per_mod, gate_idx, tm, tn):
    t, k = a.shape
    n = w.shape[1]
    per = rows_per_mod // tm
    return pl.pallas_call(
        functools.partial(_mm_res_kernel, gate_idx=gate_idx),
        grid=(t // tm, n // tn),
        in_specs=[
            pl.BlockSpec((tm, k), lambda i, j: (i, 0)),
            pl.BlockSpec((k, tn), lambda i, j: (0, j)),
            pl.BlockSpec((tm, tn), lambda i, j: (i, j)),
            pl.BlockSpec((1, 6, tn), lambda i, j: (i // per, 0, j)),
        ],
        out_specs=pl.BlockSpec((tm, tn), lambda i, j: (i, j)),
        out_shape=jax.ShapeDtypeStruct((t, n), F32),
        compiler_params=_cp("arbitrary", "arbitrary"),
        name="matmul_residual",
    )(a, w, x, mod)


def _rope_tile(x, cos, sin_signed, first_half):
    swapped = jnp.where(first_half, pltpu.roll(x, LANES - 16, 1), pltpu.roll(x, 16, 1))
    return x * cos + swapped * sin_signed


def _prep_kernel(q_ref, k_ref, v_ref, ck_ref, cv_ref, cos_ref, sin_ref, qo_ref, ko_ref, vo_ref):
    r = pl.program_id(1)

    @pl.when(r == 0)
    def _():
        ko_ref[0] = ck_ref[0, 0].astype(BF16)
        vo_ref[0] = cv_ref[0, 0].astype(BF16)

    @pl.when(r > 0)
    def _():
        cos = cos_ref[...]
        sin = sin_ref[...]
        lane = lax.broadcasted_iota(jnp.int32, cos.shape, 1)
        first_half = (lane % 32) < 16
        for h in range(QK_A // LANES):
            sl = slice(h * LANES, (h + 1) * LANES)
            qo_ref[0, :, sl] = _rope_tile(q_ref[:, sl], cos, sin, first_half).astype(BF16)
            ko_ref[0, :, sl] = _rope_tile(k_ref[:, sl], cos, sin, first_half).astype(BF16)
        vo_ref[0] = v_ref[...].astype(BF16)


def _prep_sample(p, cache_k, cache_v, layer, cos_t, sin_t, b, n):
    past = cache_k.shape[2]
    tr = past
    nr = n // tr
    blk = QK_A

    def prow(bi, r):
        return bi * nr + jnp.maximum(r - 1, 0)

    return pl.pallas_call(
        _prep_kernel,
        grid=(b, nr + 1),
        in_specs=[
            pl.BlockSpec((tr, blk), lambda bi, r: (prow(bi, r), OFF_QA // blk)),
            pl.BlockSpec((tr, blk), lambda bi, r: (prow(bi, r), OFF_KA // blk)),
            pl.BlockSpec((tr, blk), lambda bi, r: (prow(bi, r), OFF_VA // blk)),
            pl.BlockSpec((1, 1, past, blk), lambda bi, r: (bi, layer, 0, 0)),
            pl.BlockSpec((1, 1, past, blk), lambda bi, r: (bi, layer, 0, 0)),
            pl.BlockSpec((tr, LANES), lambda bi, r: (jnp.maximum(r - 1, 0), 0)),
            pl.BlockSpec((tr, LANES), lambda bi, r: (jnp.maximum(r - 1, 0), 0)),
        ],
        out_specs=[
            pl.BlockSpec((1, tr, blk), lambda bi, r: (bi, jnp.maximum(r - 1, 0), 0)),
            pl.BlockSpec((1, tr, blk), lambda bi, r: (bi, r, 0)),
            pl.BlockSpec((1, tr, blk), lambda bi, r: (bi, r, 0)),
        ],
        out_shape=[
            jax.ShapeDtypeStruct((b, n, blk), BF16),
            jax.ShapeDtypeStruct((b, past + n, blk), BF16),
            jax.ShapeDtypeStruct((b, past + n, blk), BF16),
        ],
        compiler_params=_cp("arbitrary", "arbitrary"),
        name="rope_prep",
    )(p, p, p, cache_k, cache_v, cos_t, sin_t)


def _rope_tables(n):
    rows = n // GRID_W
    row = jnp.repeat(jnp.arange(rows, dtype=F32), GRID_W)
    col = jnp.tile(jnp.arange(GRID_W, dtype=F32), rows)
    n_freq = DH_A // 4
    inv = ROPE_THETA ** (-jnp.arange(n_freq, dtype=F32) / n_freq)
    ang_r = row[:, None] * inv
    ang_c = col[:, None] * inv
    cos64 = jnp.concatenate([jnp.cos(ang_r), jnp.cos(ang_r), jnp.cos(ang_c), jnp.cos(ang_c)], -1)
    sin64 = jnp.concatenate([-jnp.sin(ang_r), jnp.sin(ang_r), -jnp.sin(ang_c), jnp.sin(ang_c)], -1)
    return jnp.tile(cos64, (1, 2)), jnp.tile(sin64, (1, 2))


def _softmax_rows(s):
    m = jnp.max(s, axis=-1, keepdims=True)
    e = jnp.exp(s - m)
    return e, 1.0 / jnp.sum(e, axis=-1, keepdims=True)


def _attn_kernel(q_ref, k_ref, v_ref, lam_ref, g_ref, o_ref, *, lam_init):
    lp = lam_ref[...]
    lam = (jnp.exp(jnp.sum(lp[0:1] * lp[1:2], axis=-1, keepdims=True))
           - jnp.exp(jnp.sum(lp[2:3] * lp[3:4], axis=-1, keepdims=True)) + lam_init)
    q = (q_ref[0].astype(F32) * (DH_A ** -0.5)).astype(BF16)
    k = k_ref[0].astype(BF16)
    v = v_ref[0].astype(BF16)
    lane = lax.broadcasted_iota(jnp.int32, q.shape, 1)
    zero = jnp.zeros_like(q)
    e1, r1 = _softmax_rows(_dot_nt(jnp.where(lane < DH_A, q, zero), k))
    e2, r2 = _softmax_rows(_dot_nt(jnp.where(lane >= DH_A, q, zero), k))
    w = e1 * r1 - e2 * (lam * r2)
    o = _dot(w.astype(BF16), v)
    o = o * lax.rsqrt(jnp.mean(o * o, axis=-1, keepdims=True) + EPS)
    o_ref[0] = (o * g_ref[...] * (1.0 - lam_init)).astype(o_ref.dtype)


def _diff_attention(q, k, v, q_off, k_off, v_off, attn_lambda, subln_g, layer_idx, tq):
    b, nq = q.shape[0], q.shape[1]
    nk = k.shape[1]
    lam_init = 0.8 - 0.6 * math.exp(-0.3 * layer_idx)
    return pl.pallas_call(
        functools.partial(_attn_kernel, lam_init=lam_init),
        grid=(b, N_HEADS_A, nq // tq),
        in_specs=[
            pl.BlockSpec((1, tq, LANES), lambda bi, h, i: (bi, i, q_off + h)),
            pl.BlockSpec((1, nk, LANES), lambda bi, h, i: (bi, 0, k_off + h)),
            pl.BlockSpec((1, nk, LANES), lambda bi, h, i: (bi, 0, v_off + h)),
            pl.BlockSpec((4, DH_A), lambda bi, h, i: (0, 0)),
            pl.BlockSpec((1, DV_A), lambda bi, h, i: (0, 0)),
        ],
        out_specs=pl.BlockSpec((1, tq, LANES), lambda bi, h, i: (bi, i, h)),
        out_shape=jax.ShapeDtypeStruct((b, nq, WIDTH_A), BF16),
        compiler_params=_cp("arbitrary", "arbitrary", "arbitrary"),
        name="diff_attention",
    )(q, k, v, attn_lambda, subln_g.reshape(1, DV_A))


def _ret_kernel(ld_ref, q_ref, k_ref, v_ref, gf_ref, gb_ref, *rest, n, c, has_s0):
    if has_s0:
        s0_ref, o_ref, sfin_ref, mask_ref, dec_ref, cdec_ref, state_ref, of_ref = rest
    else:
        o_ref, sfin_ref, mask_ref, dec_ref, cdec_ref, state_ref, of_ref = rest
    nc = n // c
    hq = QK_R

    lane_head = lax.broadcasted_iota(jnp.int32, (c, hq), 1) // DK_R
    row_head = lax.broadcasted_iota(jnp.int32, (hq, c), 0) // DK_R

    @pl.when(pl.program_id(0) == 0)
    def _():
        ri = lax.broadcasted_iota(jnp.int32, (c, c), 0)
        ci = lax.broadcasted_iota(jnp.int32, (c, c), 1)
        diff = (ri - ci).astype(F32)
        rowpos = lax.broadcasted_iota(jnp.int32, (c, hq), 0).astype(F32)
        for d in range(2):
            lg_lane = jnp.zeros((c, hq), F32)
            for h in range(N_HEADS_R):
                lg = ld_ref[d, h]
                lg_lane = jnp.where(lane_head == h, lg, lg_lane)
                dist = diff if d == 0 else -diff
                mask_ref[d, h] = jnp.where(dist >= 0, jnp.exp(lg * jnp.maximum(dist, 0.0)), 0.0)
                cdec_ref[d, h] = jnp.exp(jnp.full((8, DV_R), lg * c, F32))
            if d == 0:
                dec_ref[0] = jnp.exp(lg_lane * (c - 1.0 - rowpos))
                dec_ref[1] = jnp.exp(lg_lane * (rowpos + 1.0))
            else:
                dec_ref[2] = jnp.exp(lg_lane * rowpos)
                dec_ref[3] = jnp.exp(lg_lane * (c - rowpos))

    rowh = lax.broadcasted_iota(jnp.int32, (hq, DV_R), 0) // DK_R
    for d in range(2):
        for h in range(N_HEADS_R):
            if has_s0:
                s0 = s0_ref[0, d].reshape(hq, DV_R)
                state_ref[d, h] = jnp.where(rowh == h, s0, 0.0)
            else:
                state_ref[d, h] = jnp.zeros((hq, DV_R), F32)

    def chunk(ci_, d):
        start = pl.multiple_of(ci_ * c, c)
        rows = pl.ds(start, c)
        qc = q_ref[0, rows, :]
        kc = k_ref[0, rows, :] * (DK_R ** -0.5)
        kb = kc.astype(BF16)
        kdec_t = (kc * dec_ref[2 * d]).T.astype(BF16)
        qdec = (qc * dec_ref[2 * d + 1]).astype(BF16)
        qb = qc.astype(BF16)
        outs = []
        for h in range(N_HEADS_R):
            vh = v_ref[0, rows, h * DV_R:(h + 1) * DV_R].astype(BF16)
            s = _dot_nt(jnp.where(lane_head == h, qb, jnp.zeros_like(qb)), kb)
            o = _dot((s * mask_ref[d, h]).astype(BF16), vh)
            st = state_ref[d, h]
            o = o + _dot(jnp.where(lane_head == h, qdec, jnp.zeros_like(qdec)), st.astype(BF16))
            kv = _dot(jnp.where(row_head == h, kdec_t, jnp.zeros_like(kdec_t)), vh)
            state_ref[d, h] = st * cdec_ref[d, h, 0:1, :] + kv
            outs.append(o)
        return rows, outs

    def fwd_body(ci_, carry):
        rows, outs = chunk(ci_, 0)
        for h in range(N_HEADS_R):
            of_ref[rows, h * DV_R:(h + 1) * DV_R] = outs[h]
        return carry

    lax.fori_loop(0, nc, fwd_body, 0)

    def bwd_body(i, carry):
        rows, outs = chunk(nc - 1 - i, 1)
        for h in range(N_HEADS_R):
            sl = slice(h * DV_R, (h + 1) * DV_R)
            of = of_ref[rows, sl]
            ob = outs[h]
            yf = of * lax.rsqrt(jnp.mean(of * of, axis=-1, keepdims=True) + EPS)
            yb = ob * lax.rsqrt(jnp.mean(ob * ob, axis=-1, keepdims=True) + EPS)
            y = _silu(gf_ref[0, rows, sl]) * yf + _silu(gb_ref[0, rows, sl]) * yb
            o_ref[0, rows, sl] = y.astype(o_ref.dtype)
        return carry

    lax.fori_loop(0, nc, bwd_body, 0)

    for d in range(2):
        for h in range(N_HEADS_R):
            sfin_ref[0, d, h] = state_ref[d, h, h * DK_R:(h + 1) * DK_R, :]


def _retention(p3, log_decay, s0):
    b, n = p3.shape[0], p3.shape[1]
    c = RET_CHUNK
    has_s0 = s0 is not None
    in_specs = [
        pl.BlockSpec(memory_space=pltpu.SMEM),
        pl.BlockSpec((1, n, QK_R), lambda bi: (bi, 0, OFF_QR // QK_R)),
        pl.BlockSpec((1, n, QK_R), lambda bi: (bi, 0, OFF_KR // QK_R)),
        pl.BlockSpec((1, n, WIDTH_R), lambda bi: (bi, 0, OFF_VR // WIDTH_R)),
        pl.BlockSpec((1, n, WIDTH_R), lambda bi: (bi, 0, OFF_GF // WIDTH_R)),
        pl.BlockSpec((1, n, WIDTH_R), lambda bi: (bi, 0, OFF_GB // WIDTH_R)),
    ]
    args = [log_decay, p3, p3, p3, p3, p3]
    if has_s0:
        in_specs.append(pl.BlockSpec((1, 2, N_HEADS_R, DK_R, DV_R), lambda bi: (bi, 0, 0, 0, 0)))
        args.append(s0)
    return pl.pallas_call(
        functools.partial(_ret_kernel, n=n, c=c, has_s0=has_s0),
        grid=(b,),
        in_specs=in_specs,
        out_specs=[
            pl.BlockSpec((1, n, WIDTH_R), lambda bi: (bi, 0, 0)),
            pl.BlockSpec((1, 2, N_HEADS_R, DK_R, DV_R), lambda bi: (bi, 0, 0, 0, 0)),
        ],
        out_shape=[
            jax.ShapeDtypeStruct((b, n, WIDTH_R), BF16),
            jax.ShapeDtypeStruct((b, 2, N_HEADS_R, DK_R, DV_R), F32),
        ],
        scratch_shapes=[
            pltpu.VMEM((2, N_HEADS_R, c, c), F32),
            pltpu.VMEM((4, c, QK_R), F32),
            pltpu.VMEM((2, N_HEADS_R, 8, DV_R), F32),
            pltpu.VMEM((2, N_HEADS_R, QK_R, DV_R), F32),
            pltpu.VMEM((n, WIDTH_R), F32),
        ],
        compiler_params=_cp("arbitrary"),
        name="retention",
    )(*args)


def _pool_kernel(x_ref, w_ref, sc_ref, o_ref, *, n):
    t = lax.broadcasted_iota(jnp.int32, (n, POOL_GROUP_DIM), 0)

    def down(a, s):
        return jnp.where(t >= s, pltpu.roll(a, s, 0), 0.0)

    def up(a, s):
        return jnp.where(t < n - s, pltpu.roll(a, n - s, 0), 0.0)

    for g, win in enumerate(POOL_WINDOWS):
        sl = slice(g * POOL_GROUP_DIM, (g + 1) * POOL_GROUP_DIM)
        x = x_ref[0, :, sl]
        half = win // 2
        lead = x
        trail = down(x, 1)
        w = 1
        while w < half:
            lead = lead + up(lead, w)
            trail = trail + down(trail, w)
            w *= 2
        lo = jnp.clip(t - half, 0, n)
        hi = jnp.clip(t - half + win, 0, n)
        pooled = (lead + trail) / (hi - lo).astype(F32) - x
        y = _dot(pooled.astype(BF16), w_ref[g].astype(BF16))
        o_ref[0, :, sl] = (y * sc_ref[:, sl]).astype(o_ref.dtype)


def _pool_mixer(p3, pool_w, pool_scale):
    b, n = p3.shape[0], p3.shape[1]
    return pl.pallas_call(
        functools.partial(_pool_kernel, n=n),
        grid=(b,),
        in_specs=[
            pl.BlockSpec((1, n, WIDTH_P), lambda bi: (bi, 0, OFF_POOL // WIDTH_P)),
            pl.BlockSpec((len(POOL_WINDOWS), POOL_GROUP_DIM, POOL_GROUP_DIM), lambda bi: (0, 0, 0)),
            pl.BlockSpec((1, WIDTH_P), lambda bi: (0, 0)),
        ],
        out_specs=pl.BlockSpec((1, n, WIDTH_P), lambda bi: (bi, 0, 0)),
        out_shape=jax.ShapeDtypeStruct((b, n, WIDTH_P), BF16),
        compiler_params=_cp("arbitrary"),
        name="pool_mixer",
    )(p3, pool_w, pool_scale.reshape(1, WIDTH_P))


def _ffn_up_kernel(h_ref, wg_ref, wv_ref, cw_ref, cb_ref, o_ref, *, seq):
    h = h_ref[...]
    g = _dot(h, wg_ref[...])
    v = _dot(h, wv_ref[...])
    tm = g.shape[0]
    pos = lax.broadcasted_iota(jnp.int32, g.shape, 0) % seq
    g_prev = jnp.where(pos == 0, 0.0, pltpu.roll(g, 1, 0))
    g_next = jnp.where(pos == seq - 1, 0.0, pltpu.roll(g, tm - 1, 0))
    gc = g_prev * cw_ref[0:1, :] + g * cw_ref[1:2, :] + g_next * cw_ref[2:3, :] + cb_ref[...]
    o_ref[...] = (_silu(gc) * v).astype(o_ref.dtype)


def _ffn_up(h, w_up, conv_w, conv_b, seq, tm, tn):
    t = h.shape[0]
    nj = D_FF // tn
    return pl.pallas_call(
        functools.partial(_ffn_up_kernel, seq=seq),
        grid=(t // tm, nj),
        in_specs=[
            pl.BlockSpec((tm, D_MODEL), lambda i, j: (i, 0)),
            pl.BlockSpec((D_MODEL, tn), lambda i, j: (0, j)),
            pl.BlockSpec((D_MODEL, tn), lambda i, j: (0, j + nj)),
            pl.BlockSpec((3, tn), lambda i, j: (0, j)),
            pl.BlockSpec((1, tn), lambda i, j: (0, j)),
        ],
        out_specs=pl.BlockSpec((tm, tn), lambda i, j: (i, j)),
        out_shape=jax.ShapeDtypeStruct((t, D_FF), BF16),
        compiler_params=_cp("arbitrary", "arbitrary"),
        name="ffn_up_convglu",
    )(h, w_up, w_up, conv_w, conv_b.reshape(1, D_FF))


def _trunk_layer(x, mod, layer_idx, b, n, rope, ctx, wts):
    (norm1_g, w_in, attn_lambda, subln_g, log_decay, pool_w, pool_scale, w_out, norm2_g,
     w_up, conv_w, conv_b, w_down) = wts
    h = _norm_mod(x, norm1_g, mod, n if mod.shape[0] > 1 else b * n, 0)
    p = _matmul(h, w_in, 1024, 512)
    p3 = p.reshape(b, n, IN_WIDTH)
    if rope is None:
        oa = _diff_attention(p3, p3, p3, OFF_QA // LANES, OFF_KA // LANES, OFF_VA // LANES,
                             attn_lambda, subln_g, layer_idx, tq=n)
        s0 = None
    else:
        cache_k, cache_v, state_ret = ctx
        q_r, k_all, v_all = _prep_sample(p, cache_k, cache_v, layer_idx, rope[0], rope[1], b, n)
        oa = _diff_attention(q_r, k_all, v_all, 0, 0, 0, attn_lambda, subln_g, layer_idx, tq=256)
        s0 = state_ret[:, layer_idx]
    orr, s_fin = _retention(p3, log_decay, s0)
    op = _pool_mixer(p3, pool_w, pool_scale)
    mixed = jnp.concatenate([oa, orr, op], axis=-1).reshape(b * n, MIX_WIDTH)
    rows_per_mod = n if mod.shape[0] > 1 else b * n
    x = _matmul_residual(mixed, w_out, x, mod, rows_per_mod, 2, 1024, 512)
    h2 = _norm_mod(x, norm2_g, mod, rows_per_mod, 3)
    a = _ffn_up(h2, w_up, conv_w, conv_b, n, 2048, 256)
    x = _matmul_residual(a, w_down, x, mod, rows_per_mod, 5, 512, 512)
    return x, p3, s_fin


def kernel(x_prompt, x_sample, cache_k, cache_v, state_ret, c, c_ctx, ada_w, ada_b, norm1_g, w_in,
           attn_lambda, attn_subln_g, ret_log_decay, pool_w, pool_scale, w_out, norm2_g,
           ffn_w_up, ffn_conv_w, ffn_conv_b, ffn_w_down, final_g):
    bp, sp, _ = x_prompt.shape
    bs, ss, _ = x_sample.shape
    past = cache_k.shape[2]

    cs = jnp.concatenate([c, c_ctx[None, :], jnp.zeros((16 - bs - 1, D_MODEL), F32)], axis=0)
    mod = _modulation(cs, ada_w, ada_b).reshape(DEPTH, 16, 6, D_MODEL)

    rope = _rope_tables(ss)
    ck = cache_k.reshape(bs, DEPTH, past, QK_A)
    cv = cache_v.reshape(bs, DEPTH, past, WIDTH_A)

    w_in_b = w_in.astype(BF16)
    w_out_b = w_out.astype(BF16)
    w_up_b = ffn_w_up.astype(BF16)
    w_down_b = ffn_w_down.astype(BF16)

    xp = x_prompt.reshape(bp * sp, D_MODEL)
    xs = x_sample.reshape(bs * ss, D_MODEL)
    new_k, new_v, new_s = [], [], []
    for l in range(DEPTH):
        wts = (norm1_g[l], w_in_b[l], attn_lambda[l], attn_subln_g[l], ret_log_decay[l], pool_w[l],
               pool_scale[l], w_out_b[l], norm2_g[l], w_up_b[l], ffn_conv_w[l], ffn_conv_b[l], w_down_b[l])
        xp, p3, s_l = _trunk_layer(xp, mod[l, bs:bs + 1], l, bp, sp, None, None, wts)
        new_k.append(p3[:, :, OFF_KA:OFF_KA + QK_A].reshape(bp, sp, 2 * N_HEADS_A, DH_A))
        new_v.append(p3[:, :, OFF_VA:OFF_VA + WIDTH_A].reshape(bp, sp, N_HEADS_A, DV_A))
        new_s.append(s_l)
        xs, _, _ = _trunk_layer(xs, mod[l, :bs], l, bs, ss, rope, (ck, cv, state_ret), wts)
    y_prompt = _final_norm(xp, final_g).reshape(bp, sp, D_MODEL)
    y_sample = _final_norm(xs, final_g).reshape(bs, ss, D_MODEL)
    return (y_prompt, y_sample, jnp.stack(new_k, axis=1), jnp.stack(new_v, axis=1), jnp.stack(new_s, axis=1))
```

```python
import functools
import math

import jax
import jax.numpy as jnp
from jax import lax
from jax.experimental import pallas as pl
from jax.experimental.pallas import tpu as pltpu

F32 = jnp.float32
BF16 = jnp.bfloat16

D_MODEL = 2048
DEPTH = 2
GRID_W = 64
N_HEADS_A = 8
DH_A = 64
DV_A = 128
WIDTH_A = 1024
QK_A = 1024
N_HEADS_R = 4
DK_R = 64
DV_R = 128
WIDTH_R = 512
QK_R = 256
POOL_WINDOWS = (2, 4, 8, 16)
POOL_GROUP_DIM = 128
WIDTH_P = 512
MIX_WIDTH = 2048
IN_WIDTH = 5632
D_FF = 5632
ROPE_THETA = 10000.0
EPS = 1e-6

OFF_QA, OFF_KA, OFF_VA = 0, 1024, 2048
OFF_QR, OFF_KR, OFF_VR = 3072, 3328, 3584
OFF_GF, OFF_GB, OFF_POOL = 4096, 4608, 5120

LANES = 128
VMEM_LIMIT = 52 * 1024 * 1024
RET_CHUNK = 256
DOT_ROWS = 1024


def _cp(*sem):
    return pltpu.CompilerParams(dimension_semantics=sem, vmem_limit_bytes=VMEM_LIMIT)


def _silu(x):
    return x * (1.0 / (1.0 + jnp.exp(-x)))


def _dot_nt(a, b):
    return lax.dot_general(a, b, (((1,), (1,)), ((), ())), preferred_element_type=F32)


def _dot(a, b):
    return jnp.dot(a, b, preferred_element_type=F32)


def _mod_kernel(c_ref, w_ref, b_ref, o_ref):
    a = _silu(c_ref[...]).astype(BF16)
    o_ref[0] = _dot(a, w_ref[0].astype(BF16)) + b_ref[0]


def _modulation(cs, ada_w, ada_b):
    tn = 1024
    n = ada_w.shape[-1]
    return pl.pallas_call(
        _mod_kernel,
        grid=(DEPTH, n // tn),
        in_specs=[
            pl.BlockSpec((16, D_MODEL), lambda l, j: (0, 0)),
            pl.BlockSpec((1, D_MODEL, tn), lambda l, j: (l, 0, j)),
            pl.BlockSpec((1, 1, tn), lambda l, j: (l, 0, j)),
        ],
        out_specs=pl.BlockSpec((1, 16, tn), lambda l, j: (l, 0, j)),
        out_shape=jax.ShapeDtypeStruct((DEPTH, 16, n), F32),
        compiler_params=_cp("arbitrary", "arbitrary"),
        name="modulation",
    )(cs, ada_w, ada_b.reshape(DEPTH, 1, n))


def _norm_mod_kernel(x_ref, g_ref, mod_ref, o_ref, *, shift_idx):
    x = x_ref[...]
    y = x * lax.rsqrt(jnp.mean(x * x, axis=-1, keepdims=True) + EPS)
    y = y * g_ref[...]
    shift = mod_ref[0, shift_idx:shift_idx + 1, :]
    scale = mod_ref[0, shift_idx + 1:shift_idx + 2, :]
    o_ref[...] = (y * (1.0 + scale) + shift).astype(o_ref.dtype)


def _norm_mod(x, g, mod, rows_per_mod, shift_idx):
    t = x.shape[0]
    tm = 512
    per = rows_per_mod // tm
    return pl.pallas_call(
        functools.partial(_norm_mod_kernel, shift_idx=shift_idx),
        grid=(t // tm,),
        in_specs=[
            pl.BlockSpec((tm, D_MODEL), lambda i: (i, 0)),
            pl.BlockSpec((1, D_MODEL), lambda i: (0, 0)),
            pl.BlockSpec((1, 6, D_MODEL), lambda i: (i // per, 0, 0)),
        ],
        out_specs=pl.BlockSpec((tm, D_MODEL), lambda i: (i, 0)),
        out_shape=jax.ShapeDtypeStruct((t, D_MODEL), BF16),
        compiler_params=_cp("arbitrary"),
        name="norm_mod",
    )(x, g.reshape(1, D_MODEL), mod)


def _final_norm_kernel(x_ref, g_ref, o_ref):
    x = x_ref[...]
    y = x * lax.rsqrt(jnp.mean(x * x, axis=-1, keepdims=True) + EPS)
    o_ref[...] = y * g_ref[...]


def _final_norm(x, g):
    t = x.shape[0]
    tm = 512
    return pl.pallas_call(
        _final_norm_kernel,
        grid=(t // tm,),
        in_specs=[
            pl.BlockSpec((tm, D_MODEL), lambda i: (i, 0)),
            pl.BlockSpec((1, D_MODEL), lambda i: (0, 0)),
        ],
        out_specs=pl.BlockSpec((tm, D_MODEL), lambda i: (i, 0)),
        out_shape=jax.ShapeDtypeStruct((t, D_MODEL), F32),
        compiler_params=_cp("arbitrary"),
        name="final_norm",
    )(x, g.reshape(1, D_MODEL))


def _row_chunks(tm):
    return [(r, min(DOT_ROWS, tm - r)) for r in range(0, tm, DOT_ROWS)]


def _mm_kernel(a_ref, w_ref, o_ref):
    w = w_ref[...]
    for r, m in _row_chunks(a_ref.shape[0]):
        o_ref[r:r + m, :] = _dot(a_ref[r:r + m, :], w)


def _matmul(a, w, tm, tn):
    t, k = a.shape
    n = w.shape[1]
    return pl.pallas_call(
        _mm_kernel,
        grid=(t // tm, n // tn),
        in_specs=[
            pl.BlockSpec((tm, k), lambda i, j: (i, 0)),
            pl.BlockSpec((k, tn), lambda i, j: (0, j)),
        ],
        out_specs=pl.BlockSpec((tm, tn), lambda i, j: (i, j)),
        out_shape=jax.ShapeDtypeStruct((t, n), F32),
        compiler_params=_cp("arbitrary", "arbitrary"),
        name="matmul",
    )(a, w)


def _mm_res_kernel(*refs, n_a, gate_idx):
    a_refs = refs[:n_a]
    w_ref, x_ref, mod_ref, o_ref = refs[n_a:]
    gate = mod_ref[0, gate_idx:gate_idx + 1, :]
    for r, m in _row_chunks(x_ref.shape[0]):
        acc = None
        k0 = 0
        for a_ref in a_refs:
            kw = a_ref.shape[1]
            part = _dot(a_ref[r:r + m, :], w_ref[k0:k0 + kw, :])
            acc = part if acc is None else acc + part
            k0 += kw
        o_ref[r:r + m, :] = x_ref[r:r + m, :] + gate * acc


def _matmul_residual(a_parts, w, x, mod, rows_per_mod, gate_idx, tm, tn):
    t = x.shape[0]
    k, n = w.shape
    per = rows_per_mod // tm
    return pl.pallas_call(
        functools.partial(_mm_res_kernel, n_a=len(a_parts), gate_idx=gate_idx),
        grid=(t // tm, n // tn),
        in_specs=[pl.BlockSpec((tm, a.shape[1]), lambda i, j: (i, 0)) for a in a_parts] + [
            pl.BlockSpec((k, tn), lambda i, j: (0, j)),
            pl.BlockSpec((tm, tn), lambda i, j: (i, j)),
            pl.BlockSpec((1, 6, tn), lambda i, j: (i // per, 0, j)),
        ],
        out_specs=pl.BlockSpec((tm, tn), lambda i, j: (i, j)),
        out_shape=jax.ShapeDtypeStruct((t, n), F32),
        compiler_params=_cp("arbitrary", "arbitrary"),
        name="matmul_residual",
    )(*a_parts, w, x, mod)


def _rope_tile(x, cos, sin_signed, first_half):
    swapped = jnp.where(first_half, pltpu.roll(x, LANES - 16, 1), pltpu.roll(x, 16, 1))
    return x * cos + swapped * sin_signed


def _prep_kernel(q_ref, k_ref, v_ref, ck_ref, cv_ref, cos_ref, sin_ref, qo_ref, ko_ref, vo_ref):
    r = pl.program_id(1)

    @pl.when(r == 0)
    def _():
        ko_ref[0] = ck_ref[0, 0].astype(BF16)
        vo_ref[0] = cv_ref[0, 0].astype(BF16)

    @pl.when(r > 0)
    def _():
        cos = cos_ref[...]
        sin = sin_ref[...]
        lane = lax.broadcasted_iota(jnp.int32, cos.shape, 1)
        first_half = (lane % 32) < 16
        for h in range(QK_A // LANES):
            sl = slice(h * LANES, (h + 1) * LANES)
            qo_ref[0, :, sl] = _rope_tile(q_ref[:, sl], cos, sin, first_half).astype(BF16)
            ko_ref[0, :, sl] = _rope_tile(k_ref[:, sl], cos, sin, first_half).astype(BF16)
        vo_ref[0] = v_ref[...].astype(BF16)


def _prep_sample(p, cache_k, cache_v, layer, cos_t, sin_t, b, n):
    past = cache_k.shape[2]
    tr = past
    nr = n // tr
    blk = QK_A

    def prow(bi, r):
        return bi * nr + jnp.maximum(r - 1, 0)

    return pl.pallas_call(
        _prep_kernel,
        grid=(b, nr + 1),
        in_specs=[
            pl.BlockSpec((tr, blk), lambda bi, r: (prow(bi, r), OFF_QA // blk)),
            pl.BlockSpec((tr, blk), lambda bi, r: (prow(bi, r), OFF_KA // blk)),
            pl.BlockSpec((tr, blk), lambda bi, r: (prow(bi, r), OFF_VA // blk)),
            pl.BlockSpec((1, 1, past, blk), lambda bi, r: (bi, layer, 0, 0)),
            pl.BlockSpec((1, 1, past, blk), lambda bi, r: (bi, layer, 0, 0)),
            pl.BlockSpec((tr, LANES), lambda bi, r: (jnp.maximum(r - 1, 0), 0)),
            pl.BlockSpec((tr, LANES), lambda bi, r: (jnp.maximum(r - 1, 0), 0)),
        ],
        out_specs=[
            pl.BlockSpec((1, tr, blk), lambda bi, r: (bi, jnp.maximum(r - 1, 0), 0)),
            pl.BlockSpec((1, tr, blk), lambda bi, r: (bi, r, 0)),
            pl.BlockSpec((1, tr, blk), lambda bi, r: (bi, r, 0)),
        ],
        out_shape=[
            jax.ShapeDtypeStruct((b, n, blk), BF16),
            jax.ShapeDtypeStruct((b, past + n, blk), BF16),
            jax.ShapeDtypeStruct((b, past + n, blk), BF16),
        ],
        compiler_params=_cp("arbitrary", "arbitrary"),
        name="rope_prep",
    )(p, p, p, cache_k, cache_v, cos_t, sin_t)


def _rope_tables(n):
    rows = n // GRID_W
    row = jnp.repeat(jnp.arange(rows, dtype=F32), GRID_W)
    col = jnp.tile(jnp.arange(GRID_W, dtype=F32), rows)
    n_freq = DH_A // 4
    inv = ROPE_THETA ** (-jnp.arange(n_freq, dtype=F32) / n_freq)
    ang_r = row[:, None] * inv
    ang_c = col[:, None] * inv
    cos64 = jnp.concatenate([jnp.cos(ang_r), jnp.cos(ang_r), jnp.cos(ang_c), jnp.cos(ang_c)], -1)
    sin64 = jnp.concatenate([-jnp.sin(ang_r), jnp.sin(ang_r), -jnp.sin(ang_c), jnp.sin(ang_c)], -1)
    return jnp.tile(cos64, (1, 2)), jnp.tile(sin64, (1, 2))


def _softmax_rows(s):
    m = jnp.max(s, axis=-1, keepdims=True)
    e = jnp.exp(s - m)
    return e, 1.0 / jnp.sum(e, axis=-1, keepdims=True)


def _attn_kernel(q_ref, k_ref, v_ref, lam_ref, g_ref, o_ref, *, lam_init, heads, sub):
    lp = lam_ref[...]
    lam = (jnp.exp(jnp.sum(lp[0:1] * lp[1:2], axis=-1, keepdims=True))
           - jnp.exp(jnp.sum(lp[2:3] * lp[3:4], axis=-1, keepdims=True)) + lam_init)
    tq = q_ref.shape[1]
    nk = k_ref.shape[1]
    lane = lax.broadcasted_iota(jnp.int32, (sub, LANES), 1)
    gain = g_ref[...] * (1.0 - lam_init)
    ones = jnp.ones((nk, LANES), BF16)
    for h in range(heads):
        cols = slice(h * LANES, (h + 1) * LANES)
        k = k_ref[0, :, cols].astype(BF16)
        v1 = jnp.concatenate([v_ref[0, :, cols].astype(BF16), ones], axis=1)
        for r in range(0, tq, sub):
            q = (q_ref[0, r:r + sub, cols].astype(F32) * (DH_A ** -0.5)).astype(BF16)
            zero = jnp.zeros_like(q)
            qq = jnp.concatenate([jnp.where(lane < DH_A, q, zero), jnp.where(lane >= DH_A, q, zero)], axis=0)
            s = _dot_nt(qq, k)
            e = jnp.exp(s - jnp.max(s, axis=-1, keepdims=True)).astype(BF16)
            ov = _dot(e, v1)
            p = ov[:, :LANES] / ov[:, LANES:]
            o = p[:sub] - lam * p[sub:]
            o = o * lax.rsqrt(jnp.mean(o * o, axis=-1, keepdims=True) + EPS)
            o_ref[0, r:r + sub, cols] = (o * gain).astype(o_ref.dtype)


def _diff_attention(q, k, v, q_off, k_off, v_off, attn_lambda, subln_g, layer_idx, tq, heads, sub):
    b, nq = q.shape[0], q.shape[1]
    nk = k.shape[1]
    lam_init = 0.8 - 0.6 * math.exp(-0.3 * layer_idx)
    wd = heads * LANES
    return pl.pallas_call(
        functools.partial(_attn_kernel, lam_init=lam_init, heads=heads, sub=sub),
        grid=(b, N_HEADS_A // heads, nq // tq),
        in_specs=[
            pl.BlockSpec((1, tq, wd), lambda bi, h, i: (bi, i, q_off + h)),
            pl.BlockSpec((1, nk, wd), lambda bi, h, i: (bi, 0, k_off + h)),
            pl.BlockSpec((1, nk, wd), lambda bi, h, i: (bi, 0, v_off + h)),
            pl.BlockSpec((4, DH_A), lambda bi, h, i: (0, 0)),
            pl.BlockSpec((1, DV_A), lambda bi, h, i: (0, 0)),
        ],
        out_specs=pl.BlockSpec((1, tq, wd), lambda bi, h, i: (bi, i, h)),
        out_shape=jax.ShapeDtypeStruct((b, nq, WIDTH_A), BF16),
        compiler_params=_cp("arbitrary", "arbitrary", "arbitrary"),
        name="diff_attention",
    )(q, k, v, attn_lambda, subln_g.reshape(1, DV_A))


def _ret_kernel(ld_ref, q_ref, k_ref, v_ref, gf_ref, gb_ref, *rest, n, c, has_s0):
    if has_s0:
        s0_ref, o_ref, sfin_ref, mask_ref, dec_ref, cdec_ref, state_ref, of_ref = rest
    else:
        o_ref, sfin_ref, mask_ref, dec_ref, cdec_ref, state_ref, of_ref = rest
    nc = n // c
    hq = QK_R

    lane_head = lax.broadcasted_iota(jnp.int32, (c, hq), 1) // DK_R
    row_head = lax.broadcasted_iota(jnp.int32, (hq, c), 0) // DK_R

    @pl.when(pl.program_id(0) == 0)
    def _():
        ri = lax.broadcasted_iota(jnp.int32, (c, c), 0)
        ci = lax.broadcasted_iota(jnp.int32, (c, c), 1)
        diff = (ri - ci).astype(F32)
        rowpos = lax.broadcasted_iota(jnp.int32, (c, hq), 0).astype(F32)
        for d in range(2):
            lg_lane = jnp.zeros((c, hq), F32)
            for h in range(N_HEADS_R):
                lg = ld_ref[d, h]
                lg_lane = jnp.where(lane_head == h, lg, lg_lane)
                dist = diff if d == 0 else -diff
                mask_ref[d, h] = jnp.where(dist >= 0, jnp.exp(lg * jnp.maximum(dist, 0.0)), 0.0)
                cdec_ref[d, h] = jnp.exp(jnp.full((8, DV_R), lg * c, F32))
            if d == 0:
                dec_ref[0] = jnp.exp(lg_lane * (c - 1.0 - rowpos))
                dec_ref[1] = jnp.exp(lg_lane * (rowpos + 1.0))
            else:
                dec_ref[2] = jnp.exp(lg_lane * rowpos)
                dec_ref[3] = jnp.exp(lg_lane * (c - rowpos))

    rowh = lax.broadcasted_iota(jnp.int32, (hq, DV_R), 0) // DK_R
    for d in range(2):
        for h in range(N_HEADS_R):
            if has_s0:
                s0 = s0_ref[0, d].reshape(hq, DV_R)
                state_ref[d, h] = jnp.where(rowh == h, s0, 0.0)
            else:
                state_ref[d, h] = jnp.zeros((hq, DV_R), F32)

    def chunk(ci_, d):
        start = pl.multiple_of(ci_ * c, c)
        rows = pl.ds(start, c)
        qc = q_ref[0, rows, :]
        kc = k_ref[0, rows, :] * (DK_R ** -0.5)
        kb = kc.astype(BF16)
        kdec_t = (kc * dec_ref[2 * d]).T.astype(BF16)
        qdec = (qc * dec_ref[2 * d + 1]).astype(BF16)
        qb = qc.astype(BF16)
        outs = []
        for h in range(N_HEADS_R):
            vh = v_ref[0, rows, h * DV_R:(h + 1) * DV_R].astype(BF16)
            s = _dot_nt(jnp.where(lane_head == h, qb, jnp.zeros_like(qb)), kb)
            o = _dot((s * mask_ref[d, h]).astype(BF16), vh)
            st = state_ref[d, h]
            o = o + _dot(jnp.where(lane_head == h, qdec, jnp.zeros_like(qdec)), st.astype(BF16))
            kv = _dot(jnp.where(row_head == h, kdec_t, jnp.zeros_like(kdec_t)), vh)
            state_ref[d, h] = st * cdec_ref[d, h, 0:1, :] + kv
            outs.append(o)
        return rows, outs

    def fwd_body(ci_, carry):
        rows, outs = chunk(ci_, 0)
        for h in range(N_HEADS_R):
            of_ref[rows, h * DV_R:(h + 1) * DV_R] = outs[h]
        return carry

    lax.fori_loop(0, nc, fwd_body, 0)

    def bwd_body(i, carry):
        rows, outs = chunk(nc - 1 - i, 1)
        for h in range(N_HEADS_R):
            sl = slice(h * DV_R, (h + 1) * DV_R)
            of = of_ref[rows, sl]
            ob = outs[h]
            yf = of * lax.rsqrt(jnp.mean(of * of, axis=-1, keepdims=True) + EPS)
            yb = ob * lax.rsqrt(jnp.mean(ob * ob, axis=-1, keepdims=True) + EPS)
            y = _silu(gf_ref[0, rows, sl]) * yf + _silu(gb_ref[0, rows, sl]) * yb
            o_ref[0, rows, sl] = y.astype(o_ref.dtype)
        return carry

    lax.fori_loop(0, nc, bwd_body, 0)

    for d in range(2):
        for h in range(N_HEADS_R):
            sfin_ref[0, d, h] = state_ref[d, h, h * DK_R:(h + 1) * DK_R, :]


def _retention(p3, log_decay, s0):
    b, n = p3.shape[0], p3.shape[1]
    c = RET_CHUNK
    has_s0 = s0 is not None
    in_specs = [
        pl.BlockSpec(memory_space=pltpu.SMEM),
        pl.BlockSpec((1, n, QK_R), lambda bi: (bi, 0, OFF_QR // QK_R)),
        pl.BlockSpec((1, n, QK_R), lambda bi: (bi, 0, OFF_KR // QK_R)),
        pl.BlockSpec((1, n, WIDTH_R), lambda bi: (bi, 0, OFF_VR // WIDTH_R)),
        pl.BlockSpec((1, n, WIDTH_R), lambda bi: (bi, 0, OFF_GF // WIDTH_R)),
        pl.BlockSpec((1, n, WIDTH_R), lambda bi: (bi, 0, OFF_GB // WIDTH_R)),
    ]
    args = [log_decay, p3, p3, p3, p3, p3]
    if has_s0:
        in_specs.append(pl.BlockSpec((1, 2, N_HEADS_R, DK_R, DV_R), lambda bi: (bi, 0, 0, 0, 0)))
        args.append(s0)
    return pl.pallas_call(
        functools.partial(_ret_kernel, n=n, c=c, has_s0=has_s0),
        grid=(b,),
        in_specs=in_specs,
        out_specs=[
            pl.BlockSpec((1, n, WIDTH_R), lambda bi: (bi, 0, 0)),
            pl.BlockSpec((1, 2, N_HEADS_R, DK_R, DV_R), lambda bi: (bi, 0, 0, 0, 0)),
        ],
        out_shape=[
            jax.ShapeDtypeStruct((b, n, WIDTH_R), BF16),
            jax.ShapeDtypeStruct((b, 2, N_HEADS_R, DK_R, DV_R), F32),
        ],
        scratch_shapes=[
            pltpu.VMEM((2, N_HEADS_R, c, c), F32),
            pltpu.VMEM((4, c, QK_R), F32),
            pltpu.VMEM((2, N_HEADS_R, 8, DV_R), F32),
            pltpu.VMEM((2, N_HEADS_R, QK_R, DV_R), F32),
            pltpu.VMEM((n, WIDTH_R), F32),
        ],
        compiler_params=_cp("arbitrary"),
        name="retention",
    )(*args)


def _pool_kernel(x_ref, w_ref, sc_ref, o_ref, *, n):
    t = lax.broadcasted_iota(jnp.int32, (n, POOL_GROUP_DIM), 0)

    def down(a, s):
        return jnp.where(t >= s, pltpu.roll(a, s, 0), 0.0)

    def up(a, s):
        return jnp.where(t < n - s, pltpu.roll(a, n - s, 0), 0.0)

    for g, win in enumerate(POOL_WINDOWS):
        sl = slice(g * POOL_GROUP_DIM, (g + 1) * POOL_GROUP_DIM)
        x = x_ref[0, :, sl]
        half = win // 2
        lead = x
        trail = down(x, 1)
        w = 1
        while w < half:
            lead = lead + up(lead, w)
            trail = trail + down(trail, w)
            w *= 2
        lo = jnp.clip(t - half, 0, n)
        hi = jnp.clip(t - half + win, 0, n)
        pooled = (lead + trail) / (hi - lo).astype(F32) - x
        y = _dot(pooled.astype(BF16), w_ref[g].astype(BF16))
        o_ref[0, :, sl] = (y * sc_ref[:, sl]).astype(o_ref.dtype)


def _pool_mixer(p3, pool_w, pool_scale):
    b, n = p3.shape[0], p3.shape[1]
    return pl.pallas_call(
        functools.partial(_pool_kernel, n=n),
        grid=(b,),
        in_specs=[
            pl.BlockSpec((1, n, WIDTH_P), lambda bi: (bi, 0, OFF_POOL // WIDTH_P)),
            pl.BlockSpec((len(POOL_WINDOWS), POOL_GROUP_DIM, POOL_GROUP_DIM), lambda bi: (0, 0, 0)),
            pl.BlockSpec((1, WIDTH_P), lambda bi: (0, 0)),
        ],
        out_specs=pl.BlockSpec((1, n, WIDTH_P), lambda bi: (bi, 0, 0)),
        out_shape=jax.ShapeDtypeStruct((b, n, WIDTH_P), BF16),
        compiler_params=_cp("arbitrary"),
        name="pool_mixer",
    )(p3, pool_w, pool_scale.reshape(1, WIDTH_P))


def _ffn_up_kernel(h_ref, wg_ref, wv_ref, cw_ref, cb_ref, o_ref, *, seq):
    wg = wg_ref[...]
    wv = wv_ref[...]
    chunks = _row_chunks(h_ref.shape[0])
    g = jnp.concatenate([_dot(h_ref[r:r + m, :], wg) for r, m in chunks], axis=0)
    v = jnp.concatenate([_dot(h_ref[r:r + m, :], wv) for r, m in chunks], axis=0)
    tm = g.shape[0]
    pos = lax.broadcasted_iota(jnp.int32, g.shape, 0) % seq
    g_prev = jnp.where(pos == 0, 0.0, pltpu.roll(g, 1, 0))
    g_next = jnp.where(pos == seq - 1, 0.0, pltpu.roll(g, tm - 1, 0))
    gc = g_prev * cw_ref[0:1, :] + g * cw_ref[1:2, :] + g_next * cw_ref[2:3, :] + cb_ref[...]
    o_ref[...] = (_silu(gc) * v).astype(o_ref.dtype)


def _ffn_up(h, w_up, conv_w, conv_b, seq, tm, tn):
    t = h.shape[0]
    nj = D_FF // tn
    return pl.pallas_call(
        functools.partial(_ffn_up_kernel, seq=seq),
        grid=(t // tm, nj),
        in_specs=[
            pl.BlockSpec((tm, D_MODEL), lambda i, j: (i, 0)),
            pl.BlockSpec((D_MODEL, tn), lambda i, j: (0, j)),
            pl.BlockSpec((D_MODEL, tn), lambda i, j: (0, j + nj)),
            pl.BlockSpec((3, tn), lambda i, j: (0, j)),
            pl.BlockSpec((1, tn), lambda i, j: (0, j)),
        ],
        out_specs=pl.BlockSpec((tm, tn), lambda i, j: (i, j)),
        out_shape=jax.ShapeDtypeStruct((t, D_FF), BF16),
        compiler_params=_cp("arbitrary", "arbitrary"),
        name="ffn_up_convglu",
    )(h, w_up, w_up, conv_w, conv_b.reshape(1, D_FF))


def _trunk_layer(x, mod, layer_idx, b, n, rope, ctx, wts):
    (norm1_g, w_in, attn_lambda, subln_g, log_decay, pool_w, pool_scale, w_out, norm2_g,
     w_up, conv_w, conv_b, w_down) = wts
    h = _norm_mod(x, norm1_g, mod, n if mod.shape[0] > 1 else b * n, 0)
    p = _matmul(h, w_in, 2048, 512)
    p3 = p.reshape(b, n, IN_WIDTH)
    if rope is None:
        oa = _diff_attention(p3, p3, p3, OFF_QA // WIDTH_A, OFF_KA // WIDTH_A, OFF_VA // WIDTH_A,
                             attn_lambda, subln_g, layer_idx, tq=n, heads=N_HEADS_A, sub=n)
        s0 = None
    else:
        cache_k, cache_v, state_ret = ctx
        q_r, k_all, v_all = _prep_sample(p, cache_k, cache_v, layer_idx, rope[0], rope[1], b, n)
        oa = _diff_attention(q_r, k_all, v_all, 0, 0, 0, attn_lambda, subln_g, layer_idx,
                             tq=2048, heads=1, sub=256)
        s0 = state_ret[:, layer_idx]
    orr, s_fin = _retention(p3, log_decay, s0)
    op = _pool_mixer(p3, pool_w, pool_scale)
    rows_per_mod = n if mod.shape[0] > 1 else b * n
    parts = [oa.reshape(b * n, WIDTH_A), orr.reshape(b * n, WIDTH_R), op.reshape(b * n, WIDTH_P)]
    x = _matmul_residual(parts, w_out, x, mod, rows_per_mod, 2, 2048, 512)
    h2 = _norm_mod(x, norm2_g, mod, rows_per_mod, 3)
    a = _ffn_up(h2, w_up, conv_w, conv_b, n, 2048, 256)
    x = _matmul_residual([a], w_down, x, mod, rows_per_mod, 5, 1024, 512)
    return x, p3, s_fin


def kernel(x_prompt, x_sample, cache_k, cache_v, state_ret, c, c_ctx, ada_w, ada_b, norm1_g, w_in,
           attn_lambda, attn_subln_g, ret_log_decay, pool_w, pool_scale, w_out, norm2_g,
           ffn_w_up, ffn_conv_w, ffn_conv_b, ffn_w_down, final_g):
    bp, sp, _ = x_prompt.shape
    bs, ss, _ = x_sample.shape
    past = cache_k.shape[2]

    cs = jnp.concatenate([c, c_ctx[None, :], jnp.zeros((16 - bs - 1, D_MODEL), F32)], axis=0)
    mod = _modulation(cs, ada_w, ada_b).reshape(DEPTH, 16, 6, D_MODEL)

    rope = _rope_tables(ss)
    ck = cache_k.reshape(bs, DEPTH, past, QK_A)
    cv = cache_v.reshape(bs, DEPTH, past, WIDTH_A)

    w_in_b = w_in.astype(BF16)
    w_out_b = w_out.astype(BF16)
    w_up_b = ffn_w_up.astype(BF16)
    w_down_b = ffn_w_down.astype(BF16)

    xp = x_prompt.reshape(bp * sp, D_MODEL)
    xs = x_sample.reshape(bs * ss, D_MODEL)
    new_k, new_v, new_s = [], [], []
    for l in range(DEPTH):
        wts = (norm1_g[l], w_in_b[l], attn_lambda[l], attn_subln_g[l], ret_log_decay[l], pool_w[l],
               pool_scale[l], w_out_b[l], norm2_g[l], w_up_b[l], ffn_conv_w[l], ffn_conv_b[l], w_down_b[l])
        xp, p3, s_l = _trunk_layer(xp, mod[l, bs:bs + 1], l, bp, sp, None, None, wts)
        new_k.append(p3[:, :, OFF_KA:OFF_KA + QK_A].reshape(bp, sp, 2 * N_HEADS_A, DH_A))
        new_v.append(p3[:, :, OFF_VA:OFF_VA + WIDTH_A].reshape(bp, sp, N_HEADS_A, DV_A))
        new_s.append(s_l)
        xs, _, _ = _trunk_layer(xs, mod[l, :bs], l, bs, ss, rope, (ck, cv, state_ret), wts)
    y_prompt = _final_norm(xp, final_g).reshape(bp, sp, D_MODEL)
    y_sample = _final_norm(xs, final_g).reshape(bs, ss, D_MODEL)
    return (y_prompt, y_sample, jnp.stack(new_k, axis=1), jnp.stack(new_v, axis=1), jnp.stack(new_s, axis=1))
```

```python
import functools
import math

import jax
import jax.numpy as jnp
from jax import lax
from jax.experimental import pallas as pl
from jax.experimental.pallas import tpu as pltpu

F32 = jnp.float32
BF16 = jnp.bfloat16

D_MODEL = 2048
DEPTH = 2
GRID_W = 64
N_HEADS_A = 8
DH_A = 64
DV_A = 128
WIDTH_A = 1024
QK_A = 1024
N_HEADS_R = 4
DK_R = 64
DV_R = 128
WIDTH_R = 512
QK_R = 256
POOL_WINDOWS = (2, 4, 8, 16)
POOL_GROUP_DIM = 128
WIDTH_P = 512
IN_WIDTH = 5632
D_FF = 5632
ROPE_THETA = 10000.0
EPS = 1e-6

REST_WIDTH = IN_WIDTH - 3 * WIDTH_A
R_QR, R_KR, R_VR, R_GF, R_GB, R_POOL = 0, 256, 512, 1024, 1536, 2048

SHIFT1, SCALE1, GATE1, SHIFT2, SCALE2, GATE2 = range(6)
MOD_ROWS = 16
LANES = 128
VMEM_LIMIT = 52 * 1024 * 1024
DOT_ROWS = 1024
DOT_TAIL_ROWS = 256

TM_PROJ, TN_PROJ = 2048, 512
TM_IN = 1024
TM_DOWN, TN_DOWN = 1024, 512
TM_FFN, TN_FFN = 2048, 256
TM_NORM = 512
RET_CHUNK = 256
ATT_SUB = 256


def _cp(*sem):
    return pltpu.CompilerParams(dimension_semantics=sem, vmem_limit_bytes=VMEM_LIMIT)


def _silu(x):
    return x * (1.0 / (1.0 + jnp.exp(-x)))


def _dot_nt(a, b):
    return lax.dot_general(a, b, (((1,), (1,)), ((), ())), preferred_element_type=F32)


def _dot(a, b):
    return jnp.dot(a, b, preferred_element_type=F32)


def _row_chunks(tm):
    bounds = list(range(0, tm - DOT_ROWS + 1, DOT_ROWS)) + [tm - DOT_TAIL_ROWS, tm]
    return [(a, b - a) for a, b in zip(bounds[:-1], bounds[1:])]


def _lane_tiles(width):
    return [slice(c, c + LANES) for c in range(0, width, LANES)]


def _mod_kernel(c_ref, w_ref, b_ref, o_ref):
    a = _silu(c_ref[...]).astype(BF16)
    o_ref[0] = _dot(a, w_ref[0].astype(BF16)) + b_ref[0]


def _modulation(cs, ada_w, ada_b):
    tn = 1024
    n = ada_w.shape[-1]
    return pl.pallas_call(
        _mod_kernel,
        grid=(DEPTH, n // tn),
        in_specs=[
            pl.BlockSpec((MOD_ROWS, D_MODEL), lambda l, j: (0, 0)),
            pl.BlockSpec((1, D_MODEL, tn), lambda l, j: (l, 0, j)),
            pl.BlockSpec((1, 1, tn), lambda l, j: (l, 0, j)),
        ],
        out_specs=pl.BlockSpec((1, MOD_ROWS, tn), lambda l, j: (l, 0, j)),
        out_shape=jax.ShapeDtypeStruct((DEPTH, MOD_ROWS, n), F32),
        compiler_params=_cp("arbitrary", "arbitrary"),
        name="modulation",
    )(cs, ada_w, ada_b.reshape(DEPTH, 1, n))


def _norm_mod_kernel(x_ref, g_ref, mod_ref, o_ref, *, shift_idx):
    x = x_ref[...]
    y = x * lax.rsqrt(jnp.mean(x * x, axis=-1, keepdims=True) + EPS)
    y = y * g_ref[0]
    shift = mod_ref[0, 0, shift_idx:shift_idx + 1, :]
    scale = mod_ref[0, 0, shift_idx + 1:shift_idx + 2, :]
    o_ref[...] = (y * (1.0 + scale) + shift).astype(o_ref.dtype)


def _norm_mod(x, norm_g, layer, mod, mod_row, shift_idx):
    t = x.shape[0]
    tm = TM_NORM
    return pl.pallas_call(
        functools.partial(_norm_mod_kernel, shift_idx=shift_idx),
        grid=(t // tm,),
        in_specs=[
            pl.BlockSpec((tm, D_MODEL), lambda i: (i, 0)),
            pl.BlockSpec((1, 1, D_MODEL), lambda i: (layer, 0, 0)),
            pl.BlockSpec((1, 1, 6, D_MODEL), lambda i: (layer, mod_row(i, tm), 0, 0)),
        ],
        out_specs=pl.BlockSpec((tm, D_MODEL), lambda i: (i, 0)),
        out_shape=jax.ShapeDtypeStruct((t, D_MODEL), BF16),
        compiler_params=_cp("arbitrary"),
        name="norm_mod",
    )(x, norm_g, mod)


def _final_norm_kernel(x_ref, g_ref, o_ref):
    x = x_ref[...]
    y = x * lax.rsqrt(jnp.mean(x * x, axis=-1, keepdims=True) + EPS)
    o_ref[...] = y * g_ref[...]


def _final_norm(x, g):
    t = x.shape[0]
    tm = TM_NORM
    return pl.pallas_call(
        _final_norm_kernel,
        grid=(t // tm,),
        in_specs=[
            pl.BlockSpec((tm, D_MODEL), lambda i: (i, 0)),
            pl.BlockSpec((1, D_MODEL), lambda i: (0, 0)),
        ],
        out_specs=pl.BlockSpec((tm, D_MODEL), lambda i: (i, 0)),
        out_shape=jax.ShapeDtypeStruct((t, D_MODEL), F32),
        compiler_params=_cp("arbitrary"),
        name="final_norm",
    )(x, g.reshape(1, D_MODEL))


def _rope_tile(x, cos, sin_signed, first_half):
    swapped = jnp.where(first_half, pltpu.roll(x, LANES - 16, 1), pltpu.roll(x, 16, 1))
    return x * cos + swapped * sin_signed


def _win_kernel(*refs, rope):
    if rope:
        h_ref, w_ref, cos_ref, sin_ref, q_ref, k_ref, v_ref, rest_ref = refs
    else:
        h_ref, w_ref, q_ref, k_ref, v_ref, rest_ref = refs
    j = pl.program_id(1)
    w = w_ref[0]
    tiles = _lane_tiles(w.shape[1])
    per_block = WIDTH_A // w.shape[1]

    for r0, m in _row_chunks(h_ref.shape[0]):
        rows = slice(r0, r0 + m)
        first_half = (lax.broadcasted_iota(jnp.int32, (m, LANES), 1) % 32) < 16
        val = _dot(h_ref[rows, :], w)

        def store(ref, roped):
            if roped:
                for sl in tiles:
                    t = _rope_tile(val[:, sl], cos_ref[rows, :], sin_ref[rows, :], first_half)
                    ref[rows, sl] = t.astype(ref.dtype)
            else:
                ref[rows, :] = val.astype(ref.dtype)

        @pl.when(j < per_block)
        def _():
            store(q_ref, rope)

        @pl.when((j >= per_block) & (j < 2 * per_block))
        def _():
            store(k_ref, rope)

        @pl.when((j >= 2 * per_block) & (j < 3 * per_block))
        def _():
            store(v_ref, False)

        @pl.when(j >= 3 * per_block)
        def _():
            store(rest_ref, False)


def _in_projection(h, w_in, layer, rope, kv_dtype):
    t = h.shape[0]
    tm, tn = TM_IN, TN_PROJ
    pb = WIDTH_A // tn

    def blk(first):
        return lambda i, j: (i, jnp.clip(j - first, 0, pb - 1))

    in_specs = [
        pl.BlockSpec((tm, D_MODEL), lambda i, j: (i, 0)),
        pl.BlockSpec((1, D_MODEL, tn), lambda i, j: (layer, 0, j)),
    ]
    args = [h, w_in]
    if rope is not None:
        tiles_per_seq = rope[0].shape[0] // tm
        in_specs += [pl.BlockSpec((tm, LANES), lambda i, j: (i % tiles_per_seq, 0))] * 2
        args += list(rope)
    return pl.pallas_call(
        functools.partial(_win_kernel, rope=rope is not None),
        grid=(t // tm, IN_WIDTH // tn),
        in_specs=in_specs,
        out_specs=[
            pl.BlockSpec((tm, tn), blk(0)),
            pl.BlockSpec((tm, tn), blk(pb)),
            pl.BlockSpec((tm, tn), blk(2 * pb)),
            pl.BlockSpec((tm, tn), lambda i, j: (i, jnp.clip(j - 3 * pb, 0, REST_WIDTH // tn - 1))),
        ],
        out_shape=[
            jax.ShapeDtypeStruct((t, WIDTH_A), BF16),
            jax.ShapeDtypeStruct((t, WIDTH_A), kv_dtype),
            jax.ShapeDtypeStruct((t, WIDTH_A), kv_dtype),
            jax.ShapeDtypeStruct((t, REST_WIDTH), F32),
        ],
        compiler_params=_cp("arbitrary", "arbitrary"),
        name="in_projection",
    )(*args)


def _rope_tables(n):
    rows = n // GRID_W
    row = jnp.repeat(jnp.arange(rows, dtype=F32), GRID_W)
    col = jnp.tile(jnp.arange(GRID_W, dtype=F32), rows)
    n_freq = DH_A // 4
    inv = ROPE_THETA ** (-jnp.arange(n_freq, dtype=F32) / n_freq)
    ang_r = row[:, None] * inv
    ang_c = col[:, None] * inv
    cos64 = jnp.concatenate([jnp.cos(ang_r), jnp.cos(ang_r), jnp.cos(ang_c), jnp.cos(ang_c)], -1)
    sin64 = jnp.concatenate([-jnp.sin(ang_r), jnp.sin(ang_r), -jnp.sin(ang_c), jnp.sin(ang_c)], -1)
    return jnp.tile(cos64, (1, 2)), jnp.tile(sin64, (1, 2))


def _mm_res_kernel(*refs, n_a, gate_idx):
    a_refs = refs[:n_a]
    w_ref, x_ref, mod_ref, o_ref = refs[n_a:]
    gate = mod_ref[0, 0, gate_idx:gate_idx + 1, :]
    for r0, m in _row_chunks(x_ref.shape[0]):
        rows = slice(r0, r0 + m)
        acc = None
        k0 = 0
        for a_ref in a_refs:
            kw = a_ref.shape[1]
            part = _dot(a_ref[rows, :], w_ref[0, k0:k0 + kw, :])
            acc = part if acc is None else acc + part
            k0 += kw
        o_ref[rows, :] = x_ref[rows, :] + gate * acc


def _residual_projection(a_parts, w, layer, x, mod, mod_row, gate_idx, tm, tn):
    t = x.shape[0]
    k, n = w.shape[1], w.shape[2]
    return pl.pallas_call(
        functools.partial(_mm_res_kernel, n_a=len(a_parts), gate_idx=gate_idx),
        grid=(t // tm, n // tn),
        in_specs=[pl.BlockSpec((tm, a.shape[1]), lambda i, j: (i, 0)) for a in a_parts] + [
            pl.BlockSpec((1, k, tn), lambda i, j: (layer, 0, j)),
            pl.BlockSpec((tm, tn), lambda i, j: (i, j)),
            pl.BlockSpec((1, 1, 6, tn), lambda i, j: (layer, mod_row(i, tm), 0, j)),
        ],
        out_specs=pl.BlockSpec((tm, tn), lambda i, j: (i, j)),
        out_shape=jax.ShapeDtypeStruct((t, n), F32),
        compiler_params=_cp("arbitrary", "arbitrary"),
        name="residual_projection",
    )(*a_parts, w, x, mod)


def _attn_kernel(*refs, lam_init, heads, sub, n_kv):
    q_ref = refs[0]
    k_refs = refs[1:1 + n_kv]
    v_refs = refs[1 + n_kv:1 + 2 * n_kv]
    lam_ref, g_ref, o_ref = refs[1 + 2 * n_kv:]

    def rows_of(ref, cols):
        return ref[(0,) * (len(ref.shape) - 2) + (slice(None), cols)].astype(BF16)

    lp = lam_ref[0]
    lam = (jnp.exp(jnp.sum(lp[0:1] * lp[1:2], axis=-1, keepdims=True))
           - jnp.exp(jnp.sum(lp[2:3] * lp[3:4], axis=-1, keepdims=True)) + lam_init)
    tq = q_ref.shape[1]
    nk = sum(r.shape[-2] for r in k_refs)
    lane = lax.broadcasted_iota(jnp.int32, (sub, LANES), 1)
    gain = g_ref[0] * (1.0 - lam_init)
    ones = jnp.ones((nk, LANES), BF16)
    for h in range(heads):
        cols = slice(h * LANES, (h + 1) * LANES)
        k = jnp.concatenate([rows_of(r, cols) for r in k_refs], axis=0)
        v1 = jnp.concatenate([jnp.concatenate([rows_of(r, cols) for r in v_refs], axis=0), ones], axis=1)
        for r0 in range(0, tq, sub):
            q = (q_ref[0, r0:r0 + sub, cols].astype(F32) * (DH_A ** -0.5)).astype(BF16)
            zero = jnp.zeros_like(q)
            qq = jnp.concatenate([jnp.where(lane < DH_A, q, zero), jnp.where(lane >= DH_A, q, zero)], axis=0)
            s = _dot_nt(qq, k)
            e = jnp.exp(s - jnp.max(s, axis=-1, keepdims=True)).astype(BF16)
            ov = _dot(e, v1)
            p = ov[:, :LANES] / ov[:, LANES:]
            o = p[:sub] - lam * p[sub:]
            o = o * lax.rsqrt(jnp.mean(o * o, axis=-1, keepdims=True) + EPS)
            o_ref[0, r0:r0 + sub, cols] = (o * gain).astype(o_ref.dtype)


def _diff_attention(q, kv_pieces, attn_lambda, subln_g, layer, heads, sub):
    b, nq = q.shape[0], q.shape[1]
    lam_init = 0.8 - 0.6 * math.exp(-0.3 * layer)
    wd = heads * LANES
    k_specs = [pl.BlockSpec(shape, imap) for _, _, shape, imap in kv_pieces]
    return pl.pallas_call(
        functools.partial(_attn_kernel, lam_init=lam_init, heads=heads, sub=sub, n_kv=len(kv_pieces)),
        grid=(b, N_HEADS_A // heads),
        in_specs=[pl.BlockSpec((1, nq, wd), lambda bi, h: (bi, 0, h))] + k_specs + k_specs + [
            pl.BlockSpec((1, 4, DH_A), lambda bi, h: (layer, 0, 0)),
            pl.BlockSpec((1, 1, DV_A), lambda bi, h: (layer, 0, 0)),
        ],
        out_specs=pl.BlockSpec((1, nq, wd), lambda bi, h: (bi, 0, h)),
        out_shape=jax.ShapeDtypeStruct((b, nq, WIDTH_A), BF16),
        compiler_params=_cp("arbitrary", "arbitrary"),
        name="diff_attention",
    )(q, *[p[0] for p in kv_pieces], *[p[1] for p in kv_pieces], attn_lambda, subln_g)


def _ret_kernel(ld_ref, q_ref, k_ref, v_ref, gf_ref, gb_ref, *rest, n, c, has_s0, layer):
    if has_s0:
        s0_ref, o_ref, sfin_ref, mask_ref, dec_ref, cdec_ref, state_ref, of_ref = rest
    else:
        o_ref, sfin_ref, mask_ref, dec_ref, cdec_ref, state_ref, of_ref = rest
    nc = n // c
    hq = QK_R

    lane_head = lax.broadcasted_iota(jnp.int32, (c, hq), 1) // DK_R
    row_head = lax.broadcasted_iota(jnp.int32, (hq, c), 0) // DK_R

    @pl.when(pl.program_id(0) == 0)
    def _():
        ri = lax.broadcasted_iota(jnp.int32, (c, c), 0)
        ci = lax.broadcasted_iota(jnp.int32, (c, c), 1)
        diff = (ri - ci).astype(F32)
        rowpos = lax.broadcasted_iota(jnp.int32, (c, hq), 0).astype(F32)
        for d in range(2):
            lg_lane = jnp.zeros((c, hq), F32)
            for h in range(N_HEADS_R):
                lg = ld_ref[layer, d, h]
                lg_lane = jnp.where(lane_head == h, lg, lg_lane)
                dist = diff if d == 0 else -diff
                mask_ref[d, h] = jnp.where(dist >= 0, jnp.exp(lg * jnp.maximum(dist, 0.0)), 0.0)
                cdec_ref[d, h] = jnp.exp(jnp.full((8, DV_R), lg * c, F32))
            if d == 0:
                dec_ref[0] = jnp.exp(lg_lane * (c - 1.0 - rowpos))
                dec_ref[1] = jnp.exp(lg_lane * (rowpos + 1.0))
            else:
                dec_ref[2] = jnp.exp(lg_lane * rowpos)
                dec_ref[3] = jnp.exp(lg_lane * (c - rowpos))

    rowh = lax.broadcasted_iota(jnp.int32, (hq, DV_R), 0) // DK_R
    for d in range(2):
        for h in range(N_HEADS_R):
            if has_s0:
                s0 = s0_ref[0, 0, d].reshape(hq, DV_R)
                state_ref[d, h] = jnp.where(rowh == h, s0, 0.0)
            else:
                state_ref[d, h] = jnp.zeros((hq, DV_R), F32)

    def chunk(ci_, d):
        start = pl.multiple_of(ci_ * c, c)
        rows = pl.ds(start, c)
        qc = q_ref[0, rows, :]
        kc = k_ref[0, rows, :] * (DK_R ** -0.5)
        kb = kc.astype(BF16)
        kdec_t = (kc * dec_ref[2 * d]).T.astype(BF16)
        qdec = (qc * dec_ref[2 * d + 1]).astype(BF16)
        qb = qc.astype(BF16)
        outs = []
        for h in range(N_HEADS_R):
            vh = v_ref[0, rows, h * DV_R:(h + 1) * DV_R].astype(BF16)
            s = _dot_nt(jnp.where(lane_head == h, qb, jnp.zeros_like(qb)), kb)
            o = _dot((s * mask_ref[d, h]).astype(BF16), vh)
            st = state_ref[d, h]
            o = o + _dot(jnp.where(lane_head == h, qdec, jnp.zeros_like(qdec)), st.astype(BF16))
            kv = _dot(jnp.where(row_head == h, kdec_t, jnp.zeros_like(kdec_t)), vh)
            state_ref[d, h] = st * cdec_ref[d, h, 0:1, :] + kv
            outs.append(o)
        return rows, outs

    def fwd_body(ci_, carry):
        rows, outs = chunk(ci_, 0)
        for h in range(N_HEADS_R):
            of_ref[rows, h * DV_R:(h + 1) * DV_R] = outs[h]
        return carry

    lax.fori_loop(0, nc, fwd_body, 0)

    def bwd_body(i, carry):
        rows, outs = chunk(nc - 1 - i, 1)
        for h in range(N_HEADS_R):
            sl = slice(h * DV_R, (h + 1) * DV_R)
            of = of_ref[rows, sl]
            ob = outs[h]
            yf = of * lax.rsqrt(jnp.mean(of * of, axis=-1, keepdims=True) + EPS)
            yb = ob * lax.rsqrt(jnp.mean(ob * ob, axis=-1, keepdims=True) + EPS)
            y = _silu(gf_ref[0, rows, sl]) * yf + _silu(gb_ref[0, rows, sl]) * yb
            o_ref[0, rows, sl] = y.astype(o_ref.dtype)
        return carry

    lax.fori_loop(0, nc, bwd_body, 0)

    for d in range(2):
        for h in range(N_HEADS_R):
            sfin_ref[0, d, h] = state_ref[d, h, h * DK_R:(h + 1) * DK_R, :]


def _retention(rest3, log_decay, layer, state_ret):
    b, n = rest3.shape[0], rest3.shape[1]
    c = RET_CHUNK
    has_s0 = state_ret is not None
    in_specs = [
        pl.BlockSpec(memory_space=pltpu.SMEM),
        pl.BlockSpec((1, n, QK_R), lambda bi: (bi, 0, R_QR // QK_R)),
        pl.BlockSpec((1, n, QK_R), lambda bi: (bi, 0, R_KR // QK_R)),
        pl.BlockSpec((1, n, WIDTH_R), lambda bi: (bi, 0, R_VR // WIDTH_R)),
        pl.BlockSpec((1, n, WIDTH_R), lambda bi: (bi, 0, R_GF // WIDTH_R)),
        pl.BlockSpec((1, n, WIDTH_R), lambda bi: (bi, 0, R_GB // WIDTH_R)),
    ]
    args = [log_decay, rest3, rest3, rest3, rest3, rest3]
    if has_s0:
        in_specs.append(pl.BlockSpec((1, 1, 2, N_HEADS_R, DK_R, DV_R), lambda bi: (bi, layer, 0, 0, 0, 0)))
        args.append(state_ret)
    return pl.pallas_call(
        functools.partial(_ret_kernel, n=n, c=c, has_s0=has_s0, layer=layer),
        grid=(b,),
        in_specs=in_specs,
        out_specs=[
            pl.BlockSpec((1, n, WIDTH_R), lambda bi: (bi, 0, 0)),
            pl.BlockSpec((1, 2, N_HEADS_R, DK_R, DV_R), lambda bi: (bi, 0, 0, 0, 0)),
        ],
        out_shape=[
            jax.ShapeDtypeStruct((b, n, WIDTH_R), BF16),
            jax.ShapeDtypeStruct((b, 2, N_HEADS_R, DK_R, DV_R), F32),
        ],
        scratch_shapes=[
            pltpu.VMEM((2, N_HEADS_R, c, c), F32),
            pltpu.VMEM((4, c, QK_R), F32),
            pltpu.VMEM((2, N_HEADS_R, 8, DV_R), F32),
            pltpu.VMEM((2, N_HEADS_R, QK_R, DV_R), F32),
            pltpu.VMEM((n, WIDTH_R), F32),
        ],
        compiler_params=_cp("arbitrary"),
        name="retention",
    )(*args)


def _pool_kernel(x_ref, w_ref, sc_ref, o_ref, *, n):
    t = lax.broadcasted_iota(jnp.int32, (n, POOL_GROUP_DIM), 0)

    def down(a, s):
        return jnp.where(t >= s, pltpu.roll(a, s, 0), 0.0)

    def up(a, s):
        return jnp.where(t < n - s, pltpu.roll(a, n - s, 0), 0.0)

    for g, win in enumerate(POOL_WINDOWS):
        sl = slice(g * POOL_GROUP_DIM, (g + 1) * POOL_GROUP_DIM)
        x = x_ref[0, :, sl]
        half = win // 2
        lead = x
        trail = down(x, 1)
        w = 1
        while w < half:
            lead = lead + up(lead, w)
            trail = trail + down(trail, w)
            w *= 2
        lo = jnp.clip(t - half, 0, n)
        hi = jnp.clip(t - half + win, 0, n)
        pooled = (lead + trail) / (hi - lo).astype(F32) - x
        y = _dot(pooled.astype(BF16), w_ref[0, g].astype(BF16))
        o_ref[0, :, sl] = (y * sc_ref[0, :, sl]).astype(o_ref.dtype)


def _pool_mixer(rest3, pool_w, pool_scale, layer):
    b, n = rest3.shape[0], rest3.shape[1]
    ng = len(POOL_WINDOWS)
    return pl.pallas_call(
        functools.partial(_pool_kernel, n=n),
        grid=(b,),
        in_specs=[
            pl.BlockSpec((1, n, WIDTH_P), lambda bi: (bi, 0, R_POOL // WIDTH_P)),
            pl.BlockSpec((1, ng, POOL_GROUP_DIM, POOL_GROUP_DIM), lambda bi: (layer, 0, 0, 0)),
            pl.BlockSpec((1, 1, WIDTH_P), lambda bi: (layer, 0, 0)),
        ],
        out_specs=pl.BlockSpec((1, n, WIDTH_P), lambda bi: (bi, 0, 0)),
        out_shape=jax.ShapeDtypeStruct((b, n, WIDTH_P), BF16),
        compiler_params=_cp("arbitrary"),
        name="pool_mixer",
    )(rest3, pool_w, pool_scale)


def _ffn_up_kernel(h_ref, wg_ref, wv_ref, cw_ref, cb_ref, o_ref, *, seq):
    wg = wg_ref[0]
    wv = wv_ref[0]
    chunks = _row_chunks(h_ref.shape[0])
    g = jnp.concatenate([_dot(h_ref[r:r + m, :], wg) for r, m in chunks], axis=0)
    v = jnp.concatenate([_dot(h_ref[r:r + m, :], wv) for r, m in chunks], axis=0)
    tm = g.shape[0]
    pos = lax.broadcasted_iota(jnp.int32, g.shape, 0) % seq
    g_prev = jnp.where(pos == 0, 0.0, pltpu.roll(g, 1, 0))
    g_next = jnp.where(pos == seq - 1, 0.0, pltpu.roll(g, tm - 1, 0))
    gc = g_prev * cw_ref[0, 0:1, :] + g * cw_ref[0, 1:2, :] + g_next * cw_ref[0, 2:3, :] + cb_ref[0]
    o_ref[...] = (_silu(gc) * v).astype(o_ref.dtype)


def _ffn_up(h, w_up, layer, conv_w, conv_b, seq):
    t = h.shape[0]
    tm, tn = TM_FFN, TN_FFN
    nj = D_FF // tn
    return pl.pallas_call(
        functools.partial(_ffn_up_kernel, seq=seq),
        grid=(t // tm, nj),
        in_specs=[
            pl.BlockSpec((tm, D_MODEL), lambda i, j: (i, 0)),
            pl.BlockSpec((1, D_MODEL, tn), lambda i, j: (layer, 0, j)),
            pl.BlockSpec((1, D_MODEL, tn), lambda i, j: (layer, 0, j + nj)),
            pl.BlockSpec((1, 3, tn), lambda i, j: (layer, 0, j)),
            pl.BlockSpec((1, 1, tn), lambda i, j: (layer, 0, j)),
        ],
        out_specs=pl.BlockSpec((tm, tn), lambda i, j: (i, j)),
        out_shape=jax.ShapeDtypeStruct((t, D_FF), BF16),
        compiler_params=_cp("arbitrary", "arbitrary"),
        name="ffn_up_convglu",
    )(h, w_up, w_up, conv_w, conv_b)


def _trunk_layer(x, layer, b, n, mod, mod_row, rope, ctx, wts):
    (norm1_g, w_in, attn_lambda, subln_g, log_decay, pool_w, pool_scale, w_out, norm2_g, w_up, conv_w,
     conv_b, w_down) = wts
    h = _norm_mod(x, norm1_g, layer, mod, mod_row, SHIFT1)
    if rope is None:
        q, k, v, rest = _in_projection(h, w_in, layer, None, F32)
        piece = ((1, n, WIDTH_A), lambda bi, h: (bi, 0, 0))
        kv_pieces = [(k.reshape(b, n, WIDTH_A), v.reshape(b, n, WIDTH_A)) + piece]
        heads, state_ret = N_HEADS_A, None
    else:
        cache_k, cache_v, state_ret = ctx
        past = cache_k.shape[2]
        q, k, v, rest = _in_projection(h, w_in, layer, rope, BF16)
        kv_pieces = [
            (cache_k, cache_v, (1, 1, past, LANES), lambda bi, h: (bi, layer, 0, h)),
            (k.reshape(b, n, WIDTH_A), v.reshape(b, n, WIDTH_A), (1, n, LANES), lambda bi, h: (bi, 0, h)),
        ]
        heads = 1
    oa = _diff_attention(q.reshape(b, n, WIDTH_A), kv_pieces, attn_lambda, subln_g, layer, heads,
                         min(ATT_SUB, n))
    rest3 = rest.reshape(b, n, REST_WIDTH)
    orr, s_fin = _retention(rest3, log_decay, layer, state_ret)
    op = _pool_mixer(rest3, pool_w, pool_scale, layer)
    parts = [oa.reshape(b * n, WIDTH_A), orr.reshape(b * n, WIDTH_R), op.reshape(b * n, WIDTH_P)]
    x = _residual_projection(parts, w_out, layer, x, mod, mod_row, GATE1, TM_PROJ, TN_PROJ)
    h2 = _norm_mod(x, norm2_g, layer, mod, mod_row, SHIFT2)
    a = _ffn_up(h2, w_up, layer, conv_w, conv_b, n)
    x = _residual_projection([a], w_down, layer, x, mod, mod_row, GATE2, TM_DOWN, TN_DOWN)
    return x, k, v, s_fin


def kernel(x_prompt, x_sample, cache_k, cache_v, state_ret, c, c_ctx, ada_w, ada_b, norm1_g, w_in,
           attn_lambda, attn_subln_g, ret_log_decay, pool_w, pool_scale, w_out, norm2_g,
           ffn_w_up, ffn_conv_w, ffn_conv_b, ffn_w_down, final_g):
    bp, sp, _ = x_prompt.shape
    bs, ss, _ = x_sample.shape
    past = cache_k.shape[2]
    ctx_row = bs

    cs = jnp.concatenate([c, c_ctx[None, :], jnp.zeros((MOD_ROWS - bs - 1, D_MODEL), F32)], axis=0)
    mod = _modulation(cs, ada_w, ada_b).reshape(DEPTH, MOD_ROWS, 6, D_MODEL)

    rope = _rope_tables(ss)
    ck = cache_k.reshape(bs, DEPTH, past, QK_A)
    cv = cache_v.reshape(bs, DEPTH, past, WIDTH_A)

    w_in_b = w_in.astype(BF16)
    w_out_b = w_out.astype(BF16)
    w_up_b = ffn_w_up.astype(BF16)
    w_down_b = ffn_w_down.astype(BF16)
    norm1_3 = norm1_g.reshape(DEPTH, 1, D_MODEL)
    norm2_3 = norm2_g.reshape(DEPTH, 1, D_MODEL)
    wts = (norm1_3, w_in_b, attn_lambda, attn_subln_g.reshape(DEPTH, 1, DV_A), ret_log_decay, pool_w,
           pool_scale.reshape(DEPTH, 1, WIDTH_P), w_out_b, norm2_3, w_up_b, ffn_conv_w,
           ffn_conv_b.reshape(DEPTH, 1, D_FF), w_down_b)

    def prompt_row(i, tm):
        return ctx_row

    def sample_row(i, tm):
        return (i * tm) // ss

    xp = x_prompt.reshape(bp * sp, D_MODEL)
    xs = x_sample.reshape(bs * ss, D_MODEL)
    new_k, new_v, new_s = [], [], []
    for l in range(DEPTH):
        xp, k_l, v_l, s_l = _trunk_layer(xp, l, bp, sp, mod, prompt_row, None, None, wts)
        new_k.append(k_l.reshape(bp, sp, 2 * N_HEADS_A, DH_A))
        new_v.append(v_l.reshape(bp, sp, N_HEADS_A, DV_A))
        new_s.append(s_l)
        xs, _, _, _ = _trunk_layer(xs, l, bs, ss, mod, sample_row, rope, (ck, cv, state_ret), wts)
    y_prompt = _final_norm(xp, final_g).reshape(bp, sp, D_MODEL)
    y_sample = _final_norm(xs, final_g).reshape(bs, ss, D_MODEL)
    return (y_prompt, y_sample, jnp.stack(new_k, axis=1), jnp.stack(new_v, axis=1), jnp.stack(new_s, axis=1))
```

```python
import functools
import math

import jax
import jax.numpy as jnp
from jax import lax
from jax.experimental import pallas as pl
from jax.experimental.pallas import tpu as pltpu

F32 = jnp.float32
BF16 = jnp.bfloat16

D_MODEL = 2048
DEPTH = 2
GRID_W = 64
N_HEADS_A = 8
DH_A = 64
DV_A = 128
WIDTH_A = 1024
QK_A = 1024
N_HEADS_R = 4
DK_R = 64
DV_R = 128
WIDTH_R = 512
QK_R = 256
POOL_WINDOWS = (2, 4, 8, 16)
POOL_GROUP_DIM = 128
WIDTH_P = 512
IN_WIDTH = 5632
D_FF = 5632
ROPE_THETA = 10000.0
EPS = 1e-6

C_QA, C_KA, C_VA = 0, 1024, 2048
R_QR, R_KR, R_VR, R_GF, R_GB, R_POOL = 3072, 3328, 3584, 4096, 4608, 5120

SHIFT1, SCALE1, GATE1, SHIFT2, SCALE2, GATE2 = range(6)
MOD_ROWS = 16
LANES = 128
VMEM_LIMIT = 52 * 1024 * 1024
DOT_ROWS = 1024

TM_PROJ, TN_PROJ = 2048, 512
TM_DOWN, TN_DOWN = 1024, 512
TM_FFN, TN_FFN = 2048, 256
TM_NORM = 512
NORM_ROWS = 16
RET_CHUNK = 256
ATT_SUB = 256


def _cp(*sem):
    return pltpu.CompilerParams(dimension_semantics=sem, vmem_limit_bytes=VMEM_LIMIT)


def _silu(x):
    return x * (1.0 / (1.0 + jnp.exp(-x)))


def _dot_nt(a, b):
    return lax.dot_general(a, b, (((1,), (1,)), ((), ())), preferred_element_type=F32)


def _dot(a, b):
    return jnp.dot(a, b, preferred_element_type=F32)


def _row_chunks(tm):
    return [(r, min(DOT_ROWS, tm - r)) for r in range(0, tm, DOT_ROWS)]


def _mod_kernel(c_ref, w_ref, b_ref, o_ref):
    a = _silu(c_ref[...]).astype(BF16)
    o_ref[0] = _dot(a, w_ref[0].astype(BF16)) + b_ref[0]


def _modulation(cs, ada_w, ada_b):
    tn = 1024
    n = ada_w.shape[-1]
    return pl.pallas_call(
        _mod_kernel,
        grid=(DEPTH, n // tn),
        in_specs=[
            pl.BlockSpec((MOD_ROWS, D_MODEL), lambda l, j: (0, 0)),
            pl.BlockSpec((1, D_MODEL, tn), lambda l, j: (l, 0, j)),
            pl.BlockSpec((1, 1, tn), lambda l, j: (l, 0, j)),
        ],
        out_specs=pl.BlockSpec((1, MOD_ROWS, tn), lambda l, j: (l, 0, j)),
        out_shape=jax.ShapeDtypeStruct((DEPTH, MOD_ROWS, n), F32),
        compiler_params=_cp("arbitrary", "arbitrary"),
        name="modulation",
    )(cs, ada_w, ada_b.reshape(DEPTH, 1, n))


def _norm_mod_kernel(x_ref, g_ref, mod_ref, o_ref, gain_ref, *, shift_idx):
    gain_ref[0:1, :] = g_ref[0] * (1.0 + mod_ref[0, 0, shift_idx + 1:shift_idx + 2, :])
    gain_ref[1:2, :] = mod_ref[0, 0, shift_idx:shift_idx + 1, :]

    def body(c, carry):
        rows = pl.ds(pl.multiple_of(c * NORM_ROWS, NORM_ROWS), NORM_ROWS)
        x = x_ref[rows, :]
        r = lax.rsqrt(jnp.mean(x * x, axis=-1, keepdims=True) + EPS)
        o_ref[rows, :] = ((x * r) * gain_ref[0:1, :] + gain_ref[1:2, :]).astype(o_ref.dtype)
        return carry

    lax.fori_loop(0, x_ref.shape[0] // NORM_ROWS, body, 0, unroll=4)


def _norm_mod(x, norm_g, layer, mod, mod_row, shift_idx):
    t = x.shape[0]
    tm = TM_NORM
    return pl.pallas_call(
        functools.partial(_norm_mod_kernel, shift_idx=shift_idx),
        grid=(t // tm,),
        in_specs=[
            pl.BlockSpec((tm, D_MODEL), lambda i: (i, 0)),
            pl.BlockSpec((1, 1, D_MODEL), lambda i: (layer, 0, 0)),
            pl.BlockSpec((1, 1, 6, D_MODEL), lambda i: (layer, mod_row(i, tm), 0, 0)),
        ],
        out_specs=pl.BlockSpec((tm, D_MODEL), lambda i: (i, 0)),
        out_shape=jax.ShapeDtypeStruct((t, D_MODEL), BF16),
        scratch_shapes=[pltpu.VMEM((8, D_MODEL), F32)],
        compiler_params=_cp("arbitrary"),
        name="norm_mod",
    )(x, norm_g, mod)


def _final_norm_kernel(x_ref, g_ref, o_ref):
    x = x_ref[...]
    y = x * lax.rsqrt(jnp.mean(x * x, axis=-1, keepdims=True) + EPS)
    o_ref[...] = y * g_ref[...]


def _final_norm(x, g):
    t = x.shape[0]
    tm = TM_NORM
    return pl.pallas_call(
        _final_norm_kernel,
        grid=(t // tm,),
        in_specs=[
            pl.BlockSpec((tm, D_MODEL), lambda i: (i, 0)),
            pl.BlockSpec((1, D_MODEL), lambda i: (0, 0)),
        ],
        out_specs=pl.BlockSpec((tm, D_MODEL), lambda i: (i, 0)),
        out_shape=jax.ShapeDtypeStruct((t, D_MODEL), F32),
        compiler_params=_cp("arbitrary"),
        name="final_norm",
    )(x, g.reshape(1, D_MODEL))


def _rope_tile(x, cos, sin_signed, first_half):
    swapped = jnp.where(first_half, pltpu.roll(x, LANES - 16, 1), pltpu.roll(x, 16, 1))
    return x * cos + swapped * sin_signed


def _mm_kernel(a_ref, w_ref, o_ref):
    w = w_ref[0]
    for r0, m in _row_chunks(a_ref.shape[0]):
        o_ref[r0:r0 + m, :] = _dot(a_ref[r0:r0 + m, :], w)


def _in_projection(h, w_in, layer):
    t = h.shape[0]
    tm, tn = TM_PROJ, TN_PROJ
    return pl.pallas_call(
        _mm_kernel,
        grid=(t // tm, IN_WIDTH // tn),
        in_specs=[
            pl.BlockSpec((tm, D_MODEL), lambda i, j: (i, 0)),
            pl.BlockSpec((1, D_MODEL, tn), lambda i, j: (layer, 0, j)),
        ],
        out_specs=pl.BlockSpec((tm, tn), lambda i, j: (i, j)),
        out_shape=jax.ShapeDtypeStruct((t, IN_WIDTH), F32),
        compiler_params=_cp("arbitrary", "arbitrary"),
        name="in_projection",
    )(h, w_in)


def _rope_tables(n):
    rows = n // GRID_W
    row = jnp.repeat(jnp.arange(rows, dtype=F32), GRID_W)
    col = jnp.tile(jnp.arange(GRID_W, dtype=F32), rows)
    n_freq = DH_A // 4
    inv = ROPE_THETA ** (-jnp.arange(n_freq, dtype=F32) / n_freq)
    ang_r = row[:, None] * inv
    ang_c = col[:, None] * inv
    cos64 = jnp.concatenate([jnp.cos(ang_r), jnp.cos(ang_r), jnp.cos(ang_c), jnp.cos(ang_c)], -1)
    sin64 = jnp.concatenate([-jnp.sin(ang_r), jnp.sin(ang_r), -jnp.sin(ang_c), jnp.sin(ang_c)], -1)
    return jnp.tile(cos64, (1, 2)), jnp.tile(sin64, (1, 2))


def _mm_res_kernel(*refs, n_a, gate_idx):
    a_refs = refs[:n_a]
    w_ref, x_ref, mod_ref, o_ref = refs[n_a:]
    gate = mod_ref[0, 0, gate_idx:gate_idx + 1, :]
    for r0, m in _row_chunks(x_ref.shape[0]):
        rows = slice(r0, r0 + m)
        acc = None
        k0 = 0
        for a_ref in a_refs:
            kw = a_ref.shape[1]
            part = _dot(a_ref[rows, :], w_ref[0, k0:k0 + kw, :])
            acc = part if acc is None else acc + part
            k0 += kw
        o_ref[rows, :] = x_ref[rows, :] + gate * acc


def _residual_projection(a_parts, w, layer, x, mod, mod_row, gate_idx, tm, tn):
    t = x.shape[0]
    k, n = w.shape[1], w.shape[2]
    return pl.pallas_call(
        functools.partial(_mm_res_kernel, n_a=len(a_parts), gate_idx=gate_idx),
        grid=(t // tm, n // tn),
        in_specs=[pl.BlockSpec((tm, a.shape[1]), lambda i, j: (i, 0)) for a in a_parts] + [
            pl.BlockSpec((1, k, tn), lambda i, j: (layer, 0, j)),
            pl.BlockSpec((tm, tn), lambda i, j: (i, j)),
            pl.BlockSpec((1, 1, 6, tn), lambda i, j: (layer, mod_row(i, tm), 0, j)),
        ],
        out_specs=pl.BlockSpec((tm, tn), lambda i, j: (i, j)),
        out_shape=jax.ShapeDtypeStruct((t, n), F32),
        compiler_params=_cp("arbitrary", "arbitrary"),
        name="residual_projection",
    )(*a_parts, w, x, mod)


def _attn_kernel(*refs, lam_init, heads, sub, n_kv, rope):
    q_ref = refs[0]
    k_refs = refs[1:1 + n_kv]
    v_refs = refs[1 + n_kv:1 + 2 * n_kv]
    if rope:
        cos_ref, sin_ref, lam_ref, g_ref, o_ref = refs[1 + 2 * n_kv:]
    else:
        lam_ref, g_ref, o_ref = refs[1 + 2 * n_kv:]

    def rows_of(ref, cols):
        return ref[(0,) * (len(ref.shape) - 2) + (slice(None), cols)]

    def rotated(x, r0):
        m = x.shape[0]
        first_half = (lax.broadcasted_iota(jnp.int32, (m, LANES), 1) % 32) < 16
        return _rope_tile(x, cos_ref[r0:r0 + m, :], sin_ref[r0:r0 + m, :], first_half)

    lp = lam_ref[0]
    lam = (jnp.exp(jnp.sum(lp[0:1] * lp[1:2], axis=-1, keepdims=True))
           - jnp.exp(jnp.sum(lp[2:3] * lp[3:4], axis=-1, keepdims=True)) + lam_init)
    tq = q_ref.shape[1]
    nk = sum(r.shape[-2] for r in k_refs)
    lane = lax.broadcasted_iota(jnp.int32, (sub, LANES), 1)
    gain = g_ref[0] * (1.0 - lam_init)
    ones = jnp.ones((nk, LANES), BF16)
    for h in range(heads):
        cols = slice(h * LANES, (h + 1) * LANES)
        k_parts = [rows_of(r, cols) for r in k_refs]
        if rope:
            k_parts[-1] = rotated(k_parts[-1], 0)
        k = jnp.concatenate([kp.astype(BF16) for kp in k_parts], axis=0)
        v1 = jnp.concatenate([rows_of(r, cols).astype(BF16) for r in v_refs] , axis=0)
        v1 = jnp.concatenate([v1, ones], axis=1)
        for r0 in range(0, tq, sub):
            q = q_ref[0, r0:r0 + sub, cols]
            if rope:
                q = rotated(q, r0)
            q = (q * (DH_A ** -0.5)).astype(BF16)
            zero = jnp.zeros_like(q)
            qq = jnp.concatenate([jnp.where(lane < DH_A, q, zero), jnp.where(lane >= DH_A, q, zero)], axis=0)
            s = _dot_nt(qq, k)
            e = jnp.exp(s - jnp.max(s, axis=-1, keepdims=True)).astype(BF16)
            ov = _dot(e, v1)
            p = ov[:, :LANES] / ov[:, LANES:]
            o = p[:sub] - lam * p[sub:]
            o = o * lax.rsqrt(jnp.mean(o * o, axis=-1, keepdims=True) + EPS)
            o_ref[0, r0:r0 + sub, cols] = (o * gain).astype(o_ref.dtype)


def _diff_attention(p3, k_pieces, v_pieces, rope, attn_lambda, subln_g, layer, heads, sub):
    b, nq = p3.shape[0], p3.shape[1]
    lam_init = 0.8 - 0.6 * math.exp(-0.3 * layer)
    wd = heads * LANES
    specs = [pl.BlockSpec(shape, imap) for _, shape, imap in k_pieces + v_pieces]
    args = [arr for arr, _, _ in k_pieces + v_pieces]
    if rope is not None:
        specs += [pl.BlockSpec((nq, LANES), lambda bi, h: (0, 0))] * 2
        args += list(rope)
    return pl.pallas_call(
        functools.partial(_attn_kernel, lam_init=lam_init, heads=heads, sub=sub, n_kv=len(k_pieces),
                          rope=rope is not None),
        grid=(b, N_HEADS_A // heads),
        in_specs=[pl.BlockSpec((1, nq, wd), lambda bi, h: (bi, 0, h))] + specs + [
            pl.BlockSpec((1, 4, DH_A), lambda bi, h: (layer, 0, 0)),
            pl.BlockSpec((1, 1, DV_A), lambda bi, h: (layer, 0, 0)),
        ],
        out_specs=pl.BlockSpec((1, nq, wd), lambda bi, h: (bi, 0, h)),
        out_shape=jax.ShapeDtypeStruct((b, nq, WIDTH_A), BF16),
        compiler_params=_cp("arbitrary", "arbitrary"),
        name="diff_attention",
    )(p3, *args, attn_lambda, subln_g)


def _ret_kernel(ld_ref, q_ref, k_ref, v_ref, gf_ref, gb_ref, *rest, n, c, has_s0, layer):
    if has_s0:
        s0_ref, o_ref, sfin_ref, mask_ref, dec_ref, cdec_ref, state_ref, of_ref = rest
    else:
        o_ref, sfin_ref, mask_ref, dec_ref, cdec_ref, state_ref, of_ref = rest
    nc = n // c
    hq = QK_R

    lane_head = lax.broadcasted_iota(jnp.int32, (c, hq), 1) // DK_R
    row_head = lax.broadcasted_iota(jnp.int32, (hq, c), 0) // DK_R

    @pl.when(pl.program_id(0) == 0)
    def _():
        ri = lax.broadcasted_iota(jnp.int32, (c, c), 0)
        ci = lax.broadcasted_iota(jnp.int32, (c, c), 1)
        diff = (ri - ci).astype(F32)
        rowpos = lax.broadcasted_iota(jnp.int32, (c, hq), 0).astype(F32)
        for d in range(2):
            lg_lane = jnp.zeros((c, hq), F32)
            for h in range(N_HEADS_R):
                lg = ld_ref[layer, d, h]
                lg_lane = jnp.where(lane_head == h, lg, lg_lane)
                dist = diff if d == 0 else -diff
                mask_ref[d, h] = jnp.where(dist >= 0, jnp.exp(lg * jnp.maximum(dist, 0.0)), 0.0)
                cdec_ref[d, h] = jnp.exp(jnp.full((8, DV_R), lg * c, F32))
            if d == 0:
                dec_ref[0] = jnp.exp(lg_lane * (c - 1.0 - rowpos))
                dec_ref[1] = jnp.exp(lg_lane * (rowpos + 1.0))
            else:
                dec_ref[2] = jnp.exp(lg_lane * rowpos)
                dec_ref[3] = jnp.exp(lg_lane * (c - rowpos))

    rowh = lax.broadcasted_iota(jnp.int32, (hq, DV_R), 0) // DK_R
    for d in range(2):
        for h in range(N_HEADS_R):
            if has_s0:
                s0 = s0_ref[0, 0, d].reshape(hq, DV_R)
                state_ref[d, h] = jnp.where(rowh == h, s0, 0.0)
            else:
                state_ref[d, h] = jnp.zeros((hq, DV_R), F32)

    def chunk(ci_, d):
        start = pl.multiple_of(ci_ * c, c)
        rows = pl.ds(start, c)
        qc = q_ref[0, rows, :]
        kc = k_ref[0, rows, :] * (DK_R ** -0.5)
        kb = kc.astype(BF16)
        kdec_t = (kc * dec_ref[2 * d]).T.astype(BF16)
        qdec = (qc * dec_ref[2 * d + 1]).astype(BF16)
        qb = qc.astype(BF16)
        outs = []
        for h in range(N_HEADS_R):
            vh = v_ref[0, rows, h * DV_R:(h + 1) * DV_R].astype(BF16)
            s = _dot_nt(jnp.where(lane_head == h, qb, jnp.zeros_like(qb)), kb)
            o = _dot((s * mask_ref[d, h]).astype(BF16), vh)
            st = state_ref[d, h]
            o = o + _dot(jnp.where(lane_head == h, qdec, jnp.zeros_like(qdec)), st.astype(BF16))
            kv = _dot(jnp.where(row_head == h, kdec_t, jnp.zeros_like(kdec_t)), vh)
            state_ref[d, h] = st * cdec_ref[d, h, 0:1, :] + kv
            outs.append(o)
        return rows, outs

    def fwd_body(ci_, carry):
        rows, outs = chunk(ci_, 0)
        for h in range(N_HEADS_R):
            of_ref[rows, h * DV_R:(h + 1) * DV_R] = outs[h]
        return carry

    lax.fori_loop(0, nc, fwd_body, 0)

    def bwd_body(i, carry):
        rows, outs = chunk(nc - 1 - i, 1)
        for h in range(N_HEADS_R):
            sl = slice(h * DV_R, (h + 1) * DV_R)
            of = of_ref[rows, sl]
            ob = outs[h]
            yf = of * lax.rsqrt(jnp.mean(of * of, axis=-1, keepdims=True) + EPS)
            yb = ob * lax.rsqrt(jnp.mean(ob * ob, axis=-1, keepdims=True) + EPS)
            y = _silu(gf_ref[0, rows, sl]) * yf + _silu(gb_ref[0, rows, sl]) * yb
            o_ref[0, rows, sl] = y.astype(o_ref.dtype)
        return carry

    lax.fori_loop(0, nc, bwd_body, 0)

    for d in range(2):
        for h in range(N_HEADS_R):
            sfin_ref[0, d, h] = state_ref[d, h, h * DK_R:(h + 1) * DK_R, :]


def _retention(rest3, log_decay, layer, state_ret):
    b, n = rest3.shape[0], rest3.shape[1]
    c = RET_CHUNK
    has_s0 = state_ret is not None
    in_specs = [
        pl.BlockSpec(memory_space=pltpu.SMEM),
        pl.BlockSpec((1, n, QK_R), lambda bi: (bi, 0, R_QR // QK_R)),
        pl.BlockSpec((1, n, QK_R), lambda bi: (bi, 0, R_KR // QK_R)),
        pl.BlockSpec((1, n, WIDTH_R), lambda bi: (bi, 0, R_VR // WIDTH_R)),
        pl.BlockSpec((1, n, WIDTH_R), lambda bi: (bi, 0, R_GF // WIDTH_R)),
        pl.BlockSpec((1, n, WIDTH_R), lambda bi: (bi, 0, R_GB // WIDTH_R)),
    ]
    args = [log_decay, rest3, rest3, rest3, rest3, rest3]
    if has_s0:
        in_specs.append(pl.BlockSpec((1, 1, 2, N_HEADS_R, DK_R, DV_R), lambda bi: (bi, layer, 0, 0, 0, 0)))
        args.append(state_ret)
    return pl.pallas_call(
        functools.partial(_ret_kernel, n=n, c=c, has_s0=has_s0, layer=layer),
        grid=(b,),
        in_specs=in_specs,
        out_specs=[
            pl.BlockSpec((1, n, WIDTH_R), lambda bi: (bi, 0, 0)),
            pl.BlockSpec((1, 2, N_HEADS_R, DK_R, DV_R), lambda bi: (bi, 0, 0, 0, 0)),
        ],
        out_shape=[
            jax.ShapeDtypeStruct((b, n, WIDTH_R), BF16),
            jax.ShapeDtypeStruct((b, 2, N_HEADS_R, DK_R, DV_R), F32),
        ],
        scratch_shapes=[
            pltpu.VMEM((2, N_HEADS_R, c, c), F32),
            pltpu.VMEM((4, c, QK_R), F32),
            pltpu.VMEM((2, N_HEADS_R, 8, DV_R), F32),
            pltpu.VMEM((2, N_HEADS_R, QK_R, DV_R), F32),
            pltpu.VMEM((n, WIDTH_R), F32),
        ],
        compiler_params=_cp("arbitrary"),
        name="retention",
    )(*args)


def _pool_kernel(x_ref, w_ref, sc_ref, o_ref, *, n):
    t = lax.broadcasted_iota(jnp.int32, (n, POOL_GROUP_DIM), 0)

    def down(a, s):
        return jnp.where(t >= s, pltpu.roll(a, s, 0), 0.0)

    def up(a, s):
        return jnp.where(t < n - s, pltpu.roll(a, n - s, 0), 0.0)

    for g, win in enumerate(POOL_WINDOWS):
        sl = slice(g * POOL_GROUP_DIM, (g + 1) * POOL_GROUP_DIM)
        x = x_ref[0, :, sl]
        half = win // 2
        lead = x
        trail = down(x, 1)
        w = 1
        while w < half:
            lead = lead + up(lead, w)
            trail = trail + down(trail, w)
            w *= 2
        lo = jnp.clip(t - half, 0, n)
        hi = jnp.clip(t - half + win, 0, n)
        pooled = (lead + trail) / (hi - lo).astype(F32) - x
        y = _dot(pooled.astype(BF16), w_ref[0, g].astype(BF16))
        o_ref[0, :, sl] = (y * sc_ref[0, :, sl]).astype(o_ref.dtype)


def _pool_mixer(rest3, pool_w, pool_scale, layer):
    b, n = rest3.shape[0], rest3.shape[1]
    ng = len(POOL_WINDOWS)
    return pl.pallas_call(
        functools.partial(_pool_kernel, n=n),
        grid=(b,),
        in_specs=[
            pl.BlockSpec((1, n, WIDTH_P), lambda bi: (bi, 0, R_POOL // WIDTH_P)),
            pl.BlockSpec((1, ng, POOL_GROUP_DIM, POOL_GROUP_DIM), lambda bi: (layer, 0, 0, 0)),
            pl.BlockSpec((1, 1, WIDTH_P), lambda bi: (layer, 0, 0)),
        ],
        out_specs=pl.BlockSpec((1, n, WIDTH_P), lambda bi: (bi, 0, 0)),
        out_shape=jax.ShapeDtypeStruct((b, n, WIDTH_P), BF16),
        compiler_params=_cp("arbitrary"),
        name="pool_mixer",
    )(rest3, pool_w, pool_scale)


def _ffn_up_kernel(h_ref, wg_ref, wv_ref, cw_ref, cb_ref, o_ref, *, seq):
    wg = wg_ref[0]
    wv = wv_ref[0]
    chunks = _row_chunks(h_ref.shape[0])
    g = jnp.concatenate([_dot(h_ref[r:r + m, :], wg) for r, m in chunks], axis=0)
    v = jnp.concatenate([_dot(h_ref[r:r + m, :], wv) for r, m in chunks], axis=0)
    tm = g.shape[0]
    pos = lax.broadcasted_iota(jnp.int32, g.shape, 0) % seq
    g_prev = jnp.where(pos == 0, 0.0, pltpu.roll(g, 1, 0))
    g_next = jnp.where(pos == seq - 1, 0.0, pltpu.roll(g, tm - 1, 0))
    gc = g_prev * cw_ref[0, 0:1, :] + g * cw_ref[0, 1:2, :] + g_next * cw_ref[0, 2:3, :] + cb_ref[0]
    o_ref[...] = (_silu(gc) * v).astype(o_ref.dtype)


def _ffn_up(h, w_up, layer, conv_w, conv_b, seq):
    t = h.shape[0]
    tm, tn = TM_FFN, TN_FFN
    nj = D_FF // tn
    return pl.pallas_call(
        functools.partial(_ffn_up_kernel, seq=seq),
        grid=(t // tm, nj),
        in_specs=[
            pl.BlockSpec((tm, D_MODEL), lambda i, j: (i, 0)),
            pl.BlockSpec((1, D_MODEL, tn), lambda i, j: (layer, 0, j)),
            pl.BlockSpec((1, D_MODEL, tn), lambda i, j: (layer, 0, j + nj)),
            pl.BlockSpec((1, 3, tn), lambda i, j: (layer, 0, j)),
            pl.BlockSpec((1, 1, tn), lambda i, j: (layer, 0, j)),
        ],
        out_specs=pl.BlockSpec((tm, tn), lambda i, j: (i, j)),
        out_shape=jax.ShapeDtypeStruct((t, D_FF), BF16),
        compiler_params=_cp("arbitrary", "arbitrary"),
        name="ffn_up_convglu",
    )(h, w_up, w_up, conv_w, conv_b)


def _trunk_layer(x, layer, b, n, mod, mod_row, rope, ctx, wts):
    (norm1_g, w_in, attn_lambda, subln_g, log_decay, pool_w, pool_scale, w_out, norm2_g, w_up, conv_w,
     conv_b, w_down) = wts
    h = _norm_mod(x, norm1_g, layer, mod, mod_row, SHIFT1)
    p3 = _in_projection(h, w_in, layer).reshape(b, n, IN_WIDTH)
    if rope is None:
        heads, state_ret = N_HEADS_A, None
        wd = heads * LANES
        k_pieces = [(p3, (1, n, wd), lambda bi, h: (bi, 0, C_KA // wd + h))]
        v_pieces = [(p3, (1, n, wd), lambda bi, h: (bi, 0, C_VA // wd + h))]
    else:
        cache_k, cache_v, state_ret = ctx
        past = cache_k.shape[2]
        heads = 1
        k_pieces = [(cache_k, (1, 1, past, LANES), lambda bi, h: (bi, layer, 0, h)),
                    (p3, (1, n, LANES), lambda bi, h: (bi, 0, C_KA // LANES + h))]
        v_pieces = [(cache_v, (1, 1, past, LANES), lambda bi, h: (bi, layer, 0, h)),
                    (p3, (1, n, LANES), lambda bi, h: (bi, 0, C_VA // LANES + h))]
    oa = _diff_attention(p3, k_pieces, v_pieces, rope, attn_lambda, subln_g, layer, heads, min(ATT_SUB, n))
    orr, s_fin = _retention(p3, log_decay, layer, state_ret)
    op = _pool_mixer(p3, pool_w, pool_scale, layer)
    parts = [oa.reshape(b * n, WIDTH_A), orr.reshape(b * n, WIDTH_R), op.reshape(b * n, WIDTH_P)]
    x = _residual_projection(parts, w_out, layer, x, mod, mod_row, GATE1, TM_PROJ, TN_PROJ)
    h2 = _norm_mod(x, norm2_g, layer, mod, mod_row, SHIFT2)
    a = _ffn_up(h2, w_up, layer, conv_w, conv_b, n)
    x = _residual_projection([a], w_down, layer, x, mod, mod_row, GATE2, TM_DOWN, TN_DOWN)
    return x, p3, s_fin


def kernel(x_prompt, x_sample, cache_k, cache_v, state_ret, c, c_ctx, ada_w, ada_b, norm1_g, w_in,
           attn_lambda, attn_subln_g, ret_log_decay, pool_w, pool_scale, w_out, norm2_g,
           ffn_w_up, ffn_conv_w, ffn_conv_b, ffn_w_down, final_g):
    bp, sp, _ = x_prompt.shape
    bs, ss, _ = x_sample.shape
    past = cache_k.shape[2]
    ctx_row = bs

    cs = jnp.concatenate([c, c_ctx[None, :], jnp.zeros((MOD_ROWS - bs - 1, D_MODEL), F32)], axis=0)
    mod = _modulation(cs, ada_w, ada_b).reshape(DEPTH, MOD_ROWS, 6, D_MODEL)

    rope = _rope_tables(ss)
    ck = cache_k.reshape(bs, DEPTH, past, QK_A)
    cv = cache_v.reshape(bs, DEPTH, past, WIDTH_A)

    w_in_b = w_in.astype(BF16)
    w_out_b = w_out.astype(BF16)
    w_up_b = ffn_w_up.astype(BF16)
    w_down_b = ffn_w_down.astype(BF16)
    norm1_3 = norm1_g.reshape(DEPTH, 1, D_MODEL)
    norm2_3 = norm2_g.reshape(DEPTH, 1, D_MODEL)
    wts = (norm1_3, w_in_b, attn_lambda, attn_subln_g.reshape(DEPTH, 1, DV_A), ret_log_decay, pool_w,
           pool_scale.reshape(DEPTH, 1, WIDTH_P), w_out_b, norm2_3, w_up_b, ffn_conv_w,
           ffn_conv_b.reshape(DEPTH, 1, D_FF), w_down_b)

    def prompt_row(i, tm):
        return ctx_row

    def sample_row(i, tm):
        return (i * tm) // ss

    xp = x_prompt.reshape(bp * sp, D_MODEL)
    xs = x_sample.reshape(bs * ss, D_MODEL)
    new_k, new_v, new_s = [], [], []
    for l in range(DEPTH):
        xp, p3, s_l = _trunk_layer(xp, l, bp, sp, mod, prompt_row, None, None, wts)
        new_k.append(p3[:, :, C_KA:C_KA + QK_A].reshape(bp, sp, 2 * N_HEADS_A, DH_A))
        new_v.append(p3[:, :, C_VA:C_VA + WIDTH_A].reshape(bp, sp, N_HEADS_A, DV_A))
        new_s.append(s_l)
        xs, _, _ = _trunk_layer(xs, l, bs, ss, mod, sample_row, rope, (ck, cv, state_ret), wts)
    y_prompt = _final_norm(xp, final_g).reshape(bp, sp, D_MODEL)
    y_sample = _final_norm(xs, final_g).reshape(bs, ss, D_MODEL)
    return (y_prompt, y_sample, jnp.stack(new_k, axis=1), jnp.stack(new_v, axis=1), jnp.stack(new_s, axis=1))
```

```python
import functools
import math

import jax
import jax.numpy as jnp
from jax import lax
from jax.experimental import pallas as pl
from jax.experimental.pallas import tpu as pltpu

F32 = jnp.float32
BF16 = jnp.bfloat16

D_MODEL = 2048
DEPTH = 2
GRID_W = 64
N_HEADS_A = 8
DH_A = 64
DV_A = 128
WIDTH_A = 1024
QK_A = 1024
N_HEADS_R = 4
DK_R = 64
DV_R = 128
WIDTH_R = 512
QK_R = 256
POOL_WINDOWS = (2, 4, 8, 16)
POOL_GROUP_DIM = 128
WIDTH_P = 512
IN_WIDTH = 5632
D_FF = 5632
ROPE_THETA = 10000.0
EPS = 1e-6

C_QA, C_KA, C_VA = 0, 1024, 2048
R_QR, R_KR, R_VR, R_GF, R_GB, R_POOL = 3072, 3328, 3584, 4096, 4608, 5120

SHIFT1, SCALE1, GATE1, SHIFT2, SCALE2, GATE2 = range(6)
MOD_ROWS = 16
LANES = 128
VMEM_LIMIT = 52 * 1024 * 1024
DOT_ROWS = 1024

TM_PROJ, TN_PROJ = 2048, 512
TM_DOWN, TN_DOWN = 1024, 512
TM_FFN, TN_FFN = 2048, 512
TM_NORM = 1024
NORM_ROWS = 16
RET_CHUNK = 256
ATT_SUB = 256
ATT_HEADS_LATENT = 2


def _cp(*sem):
    return pltpu.CompilerParams(dimension_semantics=sem, vmem_limit_bytes=VMEM_LIMIT)


def _silu(x):
    return x * (1.0 / (1.0 + jnp.exp(-x)))


def _dot_nt(a, b):
    return lax.dot_general(a, b, (((1,), (1,)), ((), ())), preferred_element_type=F32)


def _dot(a, b):
    return jnp.dot(a, b, preferred_element_type=F32)


def _row_chunks(tm):
    return [(r, min(DOT_ROWS, tm - r)) for r in range(0, tm, DOT_ROWS)]


def _mod_kernel(c_ref, w_ref, b_ref, o_ref):
    a = _silu(c_ref[...]).astype(BF16)
    o_ref[0] = _dot(a, w_ref[0].astype(BF16)) + b_ref[0]


def _modulation(cs, ada_w, ada_b):
    tn = 1024
    n = ada_w.shape[-1]
    return pl.pallas_call(
        _mod_kernel,
        grid=(DEPTH, n // tn),
        in_specs=[
            pl.BlockSpec((MOD_ROWS, D_MODEL), lambda l, j: (0, 0)),
            pl.BlockSpec((1, D_MODEL, tn), lambda l, j: (l, 0, j)),
            pl.BlockSpec((1, 1, tn), lambda l, j: (l, 0, j)),
        ],
        out_specs=pl.BlockSpec((1, MOD_ROWS, tn), lambda l, j: (l, 0, j)),
        out_shape=jax.ShapeDtypeStruct((DEPTH, MOD_ROWS, n), F32),
        compiler_params=_cp("arbitrary", "arbitrary"),
        name="modulation",
    )(cs, ada_w, ada_b.reshape(DEPTH, 1, n))


def _norm_mod_kernel(x_ref, g_ref, mod_ref, o_ref, gain_ref, *, shift_idx):
    gain_ref[0:1, :] = g_ref[0] * (1.0 + mod_ref[0, 0, shift_idx + 1:shift_idx + 2, :])
    gain_ref[1:2, :] = mod_ref[0, 0, shift_idx:shift_idx + 1, :]

    def body(c, carry):
        rows = pl.ds(pl.multiple_of(c * NORM_ROWS, NORM_ROWS), NORM_ROWS)
        x = x_ref[rows, :]
        r = lax.rsqrt(jnp.mean(x * x, axis=-1, keepdims=True) + EPS)
        o_ref[rows, :] = ((x * r) * gain_ref[0:1, :] + gain_ref[1:2, :]).astype(o_ref.dtype)
        return carry

    lax.fori_loop(0, x_ref.shape[0] // NORM_ROWS, body, 0, unroll=4)


def _norm_mod(x, norm_g, layer, mod, mod_row, shift_idx):
    t = x.shape[0]
    tm = TM_NORM
    return pl.pallas_call(
        functools.partial(_norm_mod_kernel, shift_idx=shift_idx),
        grid=(t // tm,),
        in_specs=[
            pl.BlockSpec((tm, D_MODEL), lambda i: (i, 0)),
            pl.BlockSpec((1, 1, D_MODEL), lambda i: (layer, 0, 0)),
            pl.BlockSpec((1, 1, 6, D_MODEL), lambda i: (layer, mod_row(i, tm), 0, 0)),
        ],
        out_specs=pl.BlockSpec((tm, D_MODEL), lambda i: (i, 0)),
        out_shape=jax.ShapeDtypeStruct((t, D_MODEL), BF16),
        scratch_shapes=[pltpu.VMEM((8, D_MODEL), F32)],
        compiler_params=_cp("arbitrary"),
        name="norm_mod",
    )(x, norm_g, mod)


def _final_norm_kernel(x_ref, g_ref, o_ref):
    x = x_ref[...]
    y = x * lax.rsqrt(jnp.mean(x * x, axis=-1, keepdims=True) + EPS)
    o_ref[...] = y * g_ref[...]


def _final_norm(x, g):
    t = x.shape[0]
    tm = TM_NORM
    return pl.pallas_call(
        _final_norm_kernel,
        grid=(t // tm,),
        in_specs=[
            pl.BlockSpec((tm, D_MODEL), lambda i: (i, 0)),
            pl.BlockSpec((1, D_MODEL), lambda i: (0, 0)),
        ],
        out_specs=pl.BlockSpec((tm, D_MODEL), lambda i: (i, 0)),
        out_shape=jax.ShapeDtypeStruct((t, D_MODEL), F32),
        compiler_params=_cp("arbitrary"),
        name="final_norm",
    )(x, g.reshape(1, D_MODEL))


def _rope_tile(x, cos, sin_signed, first_half):
    swapped = jnp.where(first_half, pltpu.roll(x, LANES - 16, 1), pltpu.roll(x, 16, 1))
    return x * cos + swapped * sin_signed


def _mm_kernel(a_ref, w_ref, o_ref):
    w = w_ref[0]
    for r0, m in _row_chunks(a_ref.shape[0]):
        o_ref[r0:r0 + m, :] = _dot(a_ref[r0:r0 + m, :], w)


def _in_projection(h, w_in, layer, col0, ncols):
    t = h.shape[0]
    tm, tn = TM_PROJ, TN_PROJ
    j0 = col0 // tn
    return pl.pallas_call(
        _mm_kernel,
        grid=(t // tm, ncols // tn),
        in_specs=[
            pl.BlockSpec((tm, D_MODEL), lambda i, j: (i, 0)),
            pl.BlockSpec((1, D_MODEL, tn), lambda i, j: (layer, 0, j0 + j)),
        ],
        out_specs=pl.BlockSpec((tm, tn), lambda i, j: (i, j)),
        out_shape=jax.ShapeDtypeStruct((t, ncols), F32),
        compiler_params=_cp("arbitrary", "arbitrary"),
        name="in_projection",
    )(h, w_in)


def _rope_tables(n):
    rows = n // GRID_W
    row = jnp.repeat(jnp.arange(rows, dtype=F32), GRID_W)
    col = jnp.tile(jnp.arange(GRID_W, dtype=F32), rows)
    n_freq = DH_A // 4
    inv = ROPE_THETA ** (-jnp.arange(n_freq, dtype=F32) / n_freq)
    ang_r = row[:, None] * inv
    ang_c = col[:, None] * inv
    cos64 = jnp.concatenate([jnp.cos(ang_r), jnp.cos(ang_r), jnp.cos(ang_c), jnp.cos(ang_c)], -1)
    sin64 = jnp.concatenate([-jnp.sin(ang_r), jnp.sin(ang_r), -jnp.sin(ang_c), jnp.sin(ang_c)], -1)
    return jnp.tile(cos64, (1, 2)), jnp.tile(sin64, (1, 2))


def _mm_res_kernel(*refs, n_a, gate_idx):
    a_refs = refs[:n_a]
    w_ref, x_ref, mod_ref, o_ref = refs[n_a:]
    gate = mod_ref[0, 0, gate_idx:gate_idx + 1, :]
    for r0, m in _row_chunks(x_ref.shape[0]):
        rows = slice(r0, r0 + m)
        acc = None
        k0 = 0
        for a_ref in a_refs:
            kw = a_ref.shape[1]
            part = _dot(a_ref[rows, :], w_ref[0, k0:k0 + kw, :])
            acc = part if acc is None else acc + part
            k0 += kw
        o_ref[rows, :] = x_ref[rows, :] + gate * acc


def _residual_projection(a_parts, w, layer, x, mod, mod_row, gate_idx, tm, tn):
    t = x.shape[0]
    k, n = w.shape[1], w.shape[2]
    return pl.pallas_call(
        functools.partial(_mm_res_kernel, n_a=len(a_parts), gate_idx=gate_idx),
        grid=(t // tm, n // tn),
        in_specs=[pl.BlockSpec((tm, a.shape[1]), lambda i, j: (i, 0)) for a in a_parts] + [
            pl.BlockSpec((1, k, tn), lambda i, j: (layer, 0, j)),
            pl.BlockSpec((tm, tn), lambda i, j: (i, j)),
            pl.BlockSpec((1, 1, 6, tn), lambda i, j: (layer, mod_row(i, tm), 0, j)),
        ],
        out_specs=pl.BlockSpec((tm, tn), lambda i, j: (i, j)),
        out_shape=jax.ShapeDtypeStruct((t, n), F32),
        compiler_params=_cp("arbitrary", "arbitrary"),
        name="residual_projection",
    )(*a_parts, w, x, mod)


def _attn_kernel(*refs, lam_init, heads, sub, n_kv, rope):
    q_ref = refs[0]
    k_refs = refs[1:1 + n_kv]
    v_refs = refs[1 + n_kv:1 + 2 * n_kv]
    if rope:
        cos_ref, sin_ref, lam_ref, g_ref, o_ref = refs[1 + 2 * n_kv:]
    else:
        lam_ref, g_ref, o_ref = refs[1 + 2 * n_kv:]

    def rows_of(ref, cols):
        return ref[(0,) * (len(ref.shape) - 2) + (slice(None), cols)]

    def rotated(x, r0):
        m = x.shape[0]
        first_half = (lax.broadcasted_iota(jnp.int32, (m, LANES), 1) % 32) < 16
        return _rope_tile(x, cos_ref[r0:r0 + m, :], sin_ref[r0:r0 + m, :], first_half)

    lp = lam_ref[0]
    lam = (jnp.exp(jnp.sum(lp[0:1] * lp[1:2], axis=-1, keepdims=True))
           - jnp.exp(jnp.sum(lp[2:3] * lp[3:4], axis=-1, keepdims=True)) + lam_init)
    tq = q_ref.shape[1]
    nk = sum(r.shape[-2] for r in k_refs)
    lane = lax.broadcasted_iota(jnp.int32, (sub, LANES), 1)
    gain = g_ref[0] * (1.0 - lam_init)
    ones = jnp.ones((nk, LANES), BF16)
    for h in range(heads):
        cols = slice(h * LANES, (h + 1) * LANES)
        k_parts = [rows_of(r, cols) for r in k_refs]
        if rope:
            k_parts[-1] = rotated(k_parts[-1], 0)
        k = jnp.concatenate([kp.astype(BF16) for kp in k_parts], axis=0)
        v1 = jnp.concatenate([rows_of(r, cols).astype(BF16) for r in v_refs] , axis=0)
        v1 = jnp.concatenate([v1, ones], axis=1)
        for r0 in range(0, tq, sub):
            q = q_ref[0, r0:r0 + sub, cols]
            if rope:
                q = rotated(q, r0)
            q = (q * (DH_A ** -0.5)).astype(BF16)
            zero = jnp.zeros_like(q)
            qq = jnp.concatenate([jnp.where(lane < DH_A, q, zero), jnp.where(lane >= DH_A, q, zero)], axis=0)
            s = _dot_nt(qq, k)
            e = jnp.exp(s - jnp.max(s, axis=-1, keepdims=True)).astype(BF16)
            ov = _dot(e, v1)
            p = ov[:, :LANES] / ov[:, LANES:]
            o = p[:sub] - lam * p[sub:]
            o = o * lax.rsqrt(jnp.mean(o * o, axis=-1, keepdims=True) + EPS)
            o_ref[0, r0:r0 + sub, cols] = (o * gain).astype(o_ref.dtype)


def _diff_attention(p3, k_pieces, v_pieces, rope, attn_lambda, subln_g, layer, heads, sub):
    b, nq = p3.shape[0], p3.shape[1]
    lam_init = 0.8 - 0.6 * math.exp(-0.3 * layer)
    wd = heads * LANES
    specs = [pl.BlockSpec(shape, imap) for _, shape, imap in k_pieces + v_pieces]
    args = [arr for arr, _, _ in k_pieces + v_pieces]
    if rope is not None:
        specs += [pl.BlockSpec((nq, LANES), lambda bi, h: (0, 0))] * 2
        args += list(rope)
    return pl.pallas_call(
        functools.partial(_attn_kernel, lam_init=lam_init, heads=heads, sub=sub, n_kv=len(k_pieces),
                          rope=rope is not None),
        grid=(b, N_HEADS_A // heads),
        in_specs=[pl.BlockSpec((1, nq, wd), lambda bi, h: (bi, 0, h))] + specs + [
            pl.BlockSpec((1, 4, DH_A), lambda bi, h: (layer, 0, 0)),
            pl.BlockSpec((1, 1, DV_A), lambda bi, h: (layer, 0, 0)),
        ],
        out_specs=pl.BlockSpec((1, nq, wd), lambda bi, h: (bi, 0, h)),
        out_shape=jax.ShapeDtypeStruct((b, nq, WIDTH_A), BF16),
        compiler_params=_cp("arbitrary", "arbitrary"),
        name="diff_attention",
    )(p3, *args, attn_lambda, subln_g)


def _ret_kernel(ld_ref, q_ref, k_ref, v_ref, gf_ref, gb_ref, *rest, n, c, has_s0, layer):
    if has_s0:
        s0_ref, o_ref, sfin_ref, mask_ref, dec_ref, cdec_ref, state_ref, of_ref = rest
    else:
        o_ref, sfin_ref, mask_ref, dec_ref, cdec_ref, state_ref, of_ref = rest
    nc = n // c
    hq = QK_R

    lane_head = lax.broadcasted_iota(jnp.int32, (c, hq), 1) // DK_R
    row_head = lax.broadcasted_iota(jnp.int32, (hq, c), 0) // DK_R

    @pl.when(pl.program_id(0) == 0)
    def _():
        ri = lax.broadcasted_iota(jnp.int32, (c, c), 0)
        ci = lax.broadcasted_iota(jnp.int32, (c, c), 1)
        diff = (ri - ci).astype(F32)
        rowpos = lax.broadcasted_iota(jnp.int32, (c, hq), 0).astype(F32)
        for d in range(2):
            lg_lane = jnp.zeros((c, hq), F32)
            for h in range(N_HEADS_R):
                lg = ld_ref[layer, d, h]
                lg_lane = jnp.where(lane_head == h, lg, lg_lane)
                dist = diff if d == 0 else -diff
                mask_ref[d, h] = jnp.where(dist >= 0, jnp.exp(lg * jnp.maximum(dist, 0.0)), 0.0)
                cdec_ref[d, h] = jnp.exp(jnp.full((8, DV_R), lg * c, F32))
            if d == 0:
                dec_ref[0] = jnp.exp(lg_lane * (c - 1.0 - rowpos))
                dec_ref[1] = jnp.exp(lg_lane * (rowpos + 1.0))
            else:
                dec_ref[2] = jnp.exp(lg_lane * rowpos)
                dec_ref[3] = jnp.exp(lg_lane * (c - rowpos))

    rowh = lax.broadcasted_iota(jnp.int32, (hq, DV_R), 0) // DK_R
    for d in range(2):
        for h in range(N_HEADS_R):
            if has_s0:
                s0 = s0_ref[0, 0, d].reshape(hq, DV_R)
                state_ref[d, h] = jnp.where(rowh == h, s0, 0.0)
            else:
                state_ref[d, h] = jnp.zeros((hq, DV_R), F32)

    def chunk(ci_, d):
        start = pl.multiple_of(ci_ * c, c)
        rows = pl.ds(start, c)
        qc = q_ref[0, rows, :]
        kc = k_ref[0, rows, :] * (DK_R ** -0.5)
        kb = kc.astype(BF16)
        kdec_t = (kc * dec_ref[2 * d]).T.astype(BF16)
        qdec = (qc * dec_ref[2 * d + 1]).astype(BF16)
        qb = qc.astype(BF16)
        outs = []
        for h in range(N_HEADS_R):
            vh = v_ref[0, rows, h * DV_R:(h + 1) * DV_R].astype(BF16)
            s = _dot_nt(jnp.where(lane_head == h, qb, jnp.zeros_like(qb)), kb)
            o = _dot((s * mask_ref[d, h]).astype(BF16), vh)
            st = state_ref[d, h]
            o = o + _dot(jnp.where(lane_head == h, qdec, jnp.zeros_like(qdec)), st.astype(BF16))
            kv = _dot(jnp.where(row_head == h, kdec_t, jnp.zeros_like(kdec_t)), vh)
            state_ref[d, h] = st * cdec_ref[d, h, 0:1, :] + kv
            outs.append(o)
        return rows, outs

    def fwd_body(ci_, carry):
        rows, outs = chunk(ci_, 0)
        for h in range(N_HEADS_R):
            of_ref[rows, h * DV_R:(h + 1) * DV_R] = outs[h]
        return carry

    lax.fori_loop(0, nc, fwd_body, 0)

    def bwd_body(i, carry):
        rows, outs = chunk(nc - 1 - i, 1)
        for h in range(N_HEADS_R):
            sl = slice(h * DV_R, (h + 1) * DV_R)
            of = of_ref[rows, sl]
            ob = outs[h]
            yf = of * lax.rsqrt(jnp.mean(of * of, axis=-1, keepdims=True) + EPS)
            yb = ob * lax.rsqrt(jnp.mean(ob * ob, axis=-1, keepdims=True) + EPS)
            y = _silu(gf_ref[0, rows, sl]) * yf + _silu(gb_ref[0, rows, sl]) * yb
            o_ref[0, rows, sl] = y.astype(o_ref.dtype)
        return carry

    lax.fori_loop(0, nc, bwd_body, 0)

    for d in range(2):
        for h in range(N_HEADS_R):
            sfin_ref[0, d, h] = state_ref[d, h, h * DK_R:(h + 1) * DK_R, :]


def _retention(rest3, base, log_decay, layer, state_ret):
    b, n = rest3.shape[0], rest3.shape[1]
    c = RET_CHUNK
    has_s0 = state_ret is not None
    in_specs = [
        pl.BlockSpec(memory_space=pltpu.SMEM),
        pl.BlockSpec((1, n, QK_R), lambda bi: (bi, 0, (R_QR - base) // QK_R)),
        pl.BlockSpec((1, n, QK_R), lambda bi: (bi, 0, (R_KR - base) // QK_R)),
        pl.BlockSpec((1, n, WIDTH_R), lambda bi: (bi, 0, (R_VR - base) // WIDTH_R)),
        pl.BlockSpec((1, n, WIDTH_R), lambda bi: (bi, 0, (R_GF - base) // WIDTH_R)),
        pl.BlockSpec((1, n, WIDTH_R), lambda bi: (bi, 0, (R_GB - base) // WIDTH_R)),
    ]
    args = [log_decay, rest3, rest3, rest3, rest3, rest3]
    if has_s0:
        in_specs.append(pl.BlockSpec((1, 1, 2, N_HEADS_R, DK_R, DV_R), lambda bi: (bi, layer, 0, 0, 0, 0)))
        args.append(state_ret)
    return pl.pallas_call(
        functools.partial(_ret_kernel, n=n, c=c, has_s0=has_s0, layer=layer),
        grid=(b,),
        in_specs=in_specs,
        out_specs=[
            pl.BlockSpec((1, n, WIDTH_R), lambda bi: (bi, 0, 0)),
            pl.BlockSpec((1, 2, N_HEADS_R, DK_R, DV_R), lambda bi: (bi, 0, 0, 0, 0)),
        ],
        out_shape=[
            jax.ShapeDtypeStruct((b, n, WIDTH_R), BF16),
            jax.ShapeDtypeStruct((b, 2, N_HEADS_R, DK_R, DV_R), F32),
        ],
        scratch_shapes=[
            pltpu.VMEM((2, N_HEADS_R, c, c), F32),
            pltpu.VMEM((4, c, QK_R), F32),
            pltpu.VMEM((2, N_HEADS_R, 8, DV_R), F32),
            pltpu.VMEM((2, N_HEADS_R, QK_R, DV_R), F32),
            pltpu.VMEM((n, WIDTH_R), F32),
        ],
        compiler_params=_cp("arbitrary"),
        name="retention",
    )(*args)


def _pool_kernel(x_ref, w_ref, sc_ref, o_ref, *, n):
    t = lax.broadcasted_iota(jnp.int32, (n, POOL_GROUP_DIM), 0)

    def down(a, s):
        return jnp.where(t >= s, pltpu.roll(a, s, 0), 0.0)

    def up(a, s):
        return jnp.where(t < n - s, pltpu.roll(a, n - s, 0), 0.0)

    for g, win in enumerate(POOL_WINDOWS):
        sl = slice(g * POOL_GROUP_DIM, (g + 1) * POOL_GROUP_DIM)
        x = x_ref[0, :, sl]
        half = win // 2
        lead = x
        trail = down(x, 1)
        w = 1
        while w < half:
            lead = lead + up(lead, w)
            trail = trail + down(trail, w)
            w *= 2
        lo = jnp.clip(t - half, 0, n)
        hi = jnp.clip(t - half + win, 0, n)
        pooled = (lead + trail) / (hi - lo).astype(F32) - x
        y = _dot(pooled.astype(BF16), w_ref[0, g].astype(BF16))
        o_ref[0, :, sl] = (y * sc_ref[0, :, sl]).astype(o_ref.dtype)


def _pool_mixer(rest3, base, pool_w, pool_scale, layer):
    b, n = rest3.shape[0], rest3.shape[1]
    ng = len(POOL_WINDOWS)
    return pl.pallas_call(
        functools.partial(_pool_kernel, n=n),
        grid=(b,),
        in_specs=[
            pl.BlockSpec((1, n, WIDTH_P), lambda bi: (bi, 0, (R_POOL - base) // WIDTH_P)),
            pl.BlockSpec((1, ng, POOL_GROUP_DIM, POOL_GROUP_DIM), lambda bi: (layer, 0, 0, 0)),
            pl.BlockSpec((1, 1, WIDTH_P), lambda bi: (layer, 0, 0)),
        ],
        out_specs=pl.BlockSpec((1, n, WIDTH_P), lambda bi: (bi, 0, 0)),
        out_shape=jax.ShapeDtypeStruct((b, n, WIDTH_P), BF16),
        compiler_params=_cp("arbitrary"),
        name="pool_mixer",
    )(rest3, pool_w, pool_scale)


def _ffn_up_kernel(h_ref, wg_ref, wv_ref, cw_ref, cb_ref, o_ref, *, seq):
    wg = wg_ref[0]
    wv = wv_ref[0]
    chunks = _row_chunks(h_ref.shape[0])
    g = jnp.concatenate([_dot(h_ref[r:r + m, :], wg) for r, m in chunks], axis=0)
    v = jnp.concatenate([_dot(h_ref[r:r + m, :], wv) for r, m in chunks], axis=0)
    tm = g.shape[0]
    pos = lax.broadcasted_iota(jnp.int32, g.shape, 0) % seq
    g_prev = jnp.where(pos == 0, 0.0, pltpu.roll(g, 1, 0))
    g_next = jnp.where(pos == seq - 1, 0.0, pltpu.roll(g, tm - 1, 0))
    gc = g_prev * cw_ref[0, 0:1, :] + g * cw_ref[0, 1:2, :] + g_next * cw_ref[0, 2:3, :] + cb_ref[0]
    o_ref[...] = (_silu(gc) * v).astype(o_ref.dtype)


def _ffn_up(h, w_up, layer, conv_w, conv_b, seq):
    t = h.shape[0]
    tm, tn = TM_FFN, TN_FFN
    nj = D_FF // tn
    return pl.pallas_call(
        functools.partial(_ffn_up_kernel, seq=seq),
        grid=(t // tm, nj),
        in_specs=[
            pl.BlockSpec((tm, D_MODEL), lambda i, j: (i, 0)),
            pl.BlockSpec((1, D_MODEL, tn), lambda i, j: (layer, 0, j)),
            pl.BlockSpec((1, D_MODEL, tn), lambda i, j: (layer, 0, j + nj)),
            pl.BlockSpec((1, 3, tn), lambda i, j: (layer, 0, j)),
            pl.BlockSpec((1, 1, tn), lambda i, j: (layer, 0, j)),
        ],
        out_specs=pl.BlockSpec((tm, tn), lambda i, j: (i, j)),
        out_shape=jax.ShapeDtypeStruct((t, D_FF), BF16),
        compiler_params=_cp("arbitrary", "arbitrary"),
        name="ffn_up_convglu",
    )(h, w_up, w_up, conv_w, conv_b)


def _trunk_layer(x, layer, b, n, mod, mod_row, rope, ctx, wts):
    (norm1_g, w_in, attn_lambda, subln_g, log_decay, pool_w, pool_scale, w_out, norm2_g, w_up, conv_w,
     conv_b, w_down) = wts
    h = _norm_mod(x, norm1_g, layer, mod, mod_row, SHIFT1)
    if rope is None:
        q3, k3, v3 = (_in_projection(h, w_in, layer, c0, WIDTH_A).reshape(b, n, WIDTH_A)
                      for c0 in (C_QA, C_KA, C_VA))
        rest_base = R_QR
        rest3 = _in_projection(h, w_in, layer, rest_base, IN_WIDTH - rest_base).reshape(b, n, -1)
        heads, state_ret = N_HEADS_A, None
        k_pieces = [(k3, (1, n, WIDTH_A), lambda bi, h: (bi, 0, 0))]
        v_pieces = [(v3, (1, n, WIDTH_A), lambda bi, h: (bi, 0, 0))]
    else:
        cache_k, cache_v, state_ret = ctx
        past = cache_k.shape[2]
        heads = ATT_HEADS_LATENT
        wd = heads * LANES
        q3 = rest3 = _in_projection(h, w_in, layer, 0, IN_WIDTH).reshape(b, n, IN_WIDTH)
        rest_base = 0
        k3 = v3 = None
        k_pieces = [(cache_k, (1, 1, past, wd), lambda bi, h: (bi, layer, 0, h)),
                    (q3, (1, n, wd), lambda bi, h: (bi, 0, C_KA // wd + h))]
        v_pieces = [(cache_v, (1, 1, past, wd), lambda bi, h: (bi, layer, 0, h)),
                    (q3, (1, n, wd), lambda bi, h: (bi, 0, C_VA // wd + h))]
    oa = _diff_attention(q3, k_pieces, v_pieces, rope, attn_lambda, subln_g, layer, heads, min(ATT_SUB, n))
    orr, s_fin = _retention(rest3, rest_base, log_decay, layer, state_ret)
    op = _pool_mixer(rest3, rest_base, pool_w, pool_scale, layer)
    parts = [oa.reshape(b * n, WIDTH_A), orr.reshape(b * n, WIDTH_R), op.reshape(b * n, WIDTH_P)]
    x = _residual_projection(parts, w_out, layer, x, mod, mod_row, GATE1, TM_PROJ, TN_PROJ)
    h2 = _norm_mod(x, norm2_g, layer, mod, mod_row, SHIFT2)
    a = _ffn_up(h2, w_up, layer, conv_w, conv_b, n)
    x = _residual_projection([a], w_down, layer, x, mod, mod_row, GATE2, TM_DOWN, TN_DOWN)
    return x, k3, v3, s_fin


def kernel(x_prompt, x_sample, cache_k, cache_v, state_ret, c, c_ctx, ada_w, ada_b, norm1_g, w_in,
           attn_lambda, attn_subln_g, ret_log_decay, pool_w, pool_scale, w_out, norm2_g,
           ffn_w_up, ffn_conv_w, ffn_conv_b, ffn_w_down, final_g):
    bp, sp, _ = x_prompt.shape
    bs, ss, _ = x_sample.shape
    past = cache_k.shape[2]
    ctx_row = bs

    cs = jnp.concatenate([c, c_ctx[None, :], jnp.zeros((MOD_ROWS - bs - 1, D_MODEL), F32)], axis=0)
    mod = _modulation(cs, ada_w, ada_b).reshape(DEPTH, MOD_ROWS, 6, D_MODEL)

    rope = _rope_tables(ss)
    ck = cache_k.reshape(bs, DEPTH, past, QK_A)
    cv = cache_v.reshape(bs, DEPTH, past, WIDTH_A)

    w_in_b = w_in.astype(BF16)
    w_out_b = w_out.astype(BF16)
    w_up_b = ffn_w_up.astype(BF16)
    w_down_b = ffn_w_down.astype(BF16)
    norm1_3 = norm1_g.reshape(DEPTH, 1, D_MODEL)
    norm2_3 = norm2_g.reshape(DEPTH, 1, D_MODEL)
    wts = (norm1_3, w_in_b, attn_lambda, attn_subln_g.reshape(DEPTH, 1, DV_A), ret_log_decay, pool_w,
           pool_scale.reshape(DEPTH, 1, WIDTH_P), w_out_b, norm2_3, w_up_b, ffn_conv_w,
           ffn_conv_b.reshape(DEPTH, 1, D_FF), w_down_b)

    def prompt_row(i, tm):
        return ctx_row

    def sample_row(i, tm):
        return (i * tm) // ss

    xp = x_prompt.reshape(bp * sp, D_MODEL)
    xs = x_sample.reshape(bs * ss, D_MODEL)
    new_k, new_v, new_s = [], [], []
    for l in range(DEPTH):
        xp, k_l, v_l, s_l = _trunk_layer(xp, l, bp, sp, mod, prompt_row, None, None, wts)
        new_k.append(k_l.reshape(bp, sp, 2 * N_HEADS_A, DH_A))
        new_v.append(v_l.reshape(bp, sp, N_HEADS_A, DV_A))
        new_s.append(s_l)
        xs, _, _, _ = _trunk_layer(xs, l, bs, ss, mod, sample_row, rope, (ck, cv, state_ret), wts)
    y_prompt = _final_norm(xp, final_g).reshape(bp, sp, D_MODEL)
    y_sample = _final_norm(xs, final_g).reshape(bs, ss, D_MODEL)
    return (y_prompt, y_sample, jnp.stack(new_k, axis=1), jnp.stack(new_v, axis=1), jnp.stack(new_s, axis=1))
```

```python
import functools
import math

import jax
import jax.numpy as jnp
from jax import lax
from jax.experimental import pallas as pl
from jax.experimental.pallas import tpu as pltpu

F32 = jnp.float32
BF16 = jnp.bfloat16

D_MODEL = 2048
DEPTH = 2
GRID_W = 64
N_HEADS_A = 8
DH_A = 64
DV_A = 128
WIDTH_A = 1024
QK_A = 1024
N_HEADS_R = 4
DK_R = 64
DV_R = 128
WIDTH_R = 512
QK_R = 256
POOL_WINDOWS = (2, 4, 8, 16)
POOL_GROUP_DIM = 128
WIDTH_P = 512
IN_WIDTH = 5632
D_FF = 5632
ROPE_THETA = 10000.0
EPS = 1e-6

C_QA, C_KA, C_VA = 0, 1024, 2048
R_QR, R_KR, R_VR, R_GF, R_GB, R_POOL = 3072, 3328, 3584, 4096, 4608, 5120

SHIFT1, SCALE1, GATE1, SHIFT2, SCALE2, GATE2 = range(6)
MOD_ROWS = 16
LANES = 128
VMEM_LIMIT = 52 * 1024 * 1024
DOT_ROWS = 1024

TM_PROJ, TN_PROJ = 2048, 512
TM_DOWN, TN_DOWN = 1024, 512
TM_FFN, TN_FFN = 2048, 512
TM_NORM = 1024
NORM_ROWS = 16
RET_CHUNK = 256
ATT_SUB = 256
ATT_HEADS_LATENT = 2


def _cp(*sem):
    return pltpu.CompilerParams(dimension_semantics=sem, vmem_limit_bytes=VMEM_LIMIT)


def _silu(x):
    return x * (1.0 / (1.0 + jnp.exp(-x)))


def _dot_nt(a, b):
    return lax.dot_general(a, b, (((1,), (1,)), ((), ())), preferred_element_type=F32)


def _dot(a, b):
    return jnp.dot(a, b, preferred_element_type=F32)


def _row_chunks(tm):
    return [(r, min(DOT_ROWS, tm - r)) for r in range(0, tm, DOT_ROWS)]


def _mod_kernel(c_ref, w_ref, b_ref, o_ref):
    a = _silu(c_ref[...]).astype(BF16)
    o_ref[0] = _dot(a, w_ref[0].astype(BF16)) + b_ref[0]


def _modulation(cs, ada_w, ada_b):
    tn = 1024
    n = ada_w.shape[-1]
    return pl.pallas_call(
        _mod_kernel,
        grid=(DEPTH, n // tn),
        in_specs=[
            pl.BlockSpec((MOD_ROWS, D_MODEL), lambda l, j: (0, 0)),
            pl.BlockSpec((1, D_MODEL, tn), lambda l, j: (l, 0, j)),
            pl.BlockSpec((1, 1, tn), lambda l, j: (l, 0, j)),
        ],
        out_specs=pl.BlockSpec((1, MOD_ROWS, tn), lambda l, j: (l, 0, j)),
        out_shape=jax.ShapeDtypeStruct((DEPTH, MOD_ROWS, n), F32),
        compiler_params=_cp("arbitrary", "arbitrary"),
        name="modulation",
    )(cs, ada_w, ada_b.reshape(DEPTH, 1, n))


def _norm_mod_kernel(x_ref, g_ref, mod_ref, o_ref, gain_ref, *, shift_idx):
    gain_ref[0:1, :] = g_ref[0] * (1.0 + mod_ref[0, 0, shift_idx + 1:shift_idx + 2, :])
    gain_ref[1:2, :] = mod_ref[0, 0, shift_idx:shift_idx + 1, :]

    def body(c, carry):
        rows = pl.ds(pl.multiple_of(c * NORM_ROWS, NORM_ROWS), NORM_ROWS)
        x = x_ref[rows, :]
        r = lax.rsqrt(jnp.mean(x * x, axis=-1, keepdims=True) + EPS)
        o_ref[rows, :] = ((x * r) * gain_ref[0:1, :] + gain_ref[1:2, :]).astype(o_ref.dtype)
        return carry

    lax.fori_loop(0, x_ref.shape[0] // NORM_ROWS, body, 0, unroll=4)


def _norm_mod(x, norm_g, layer, mod, mod_row, shift_idx):
    t = x.shape[0]
    tm = TM_NORM
    return pl.pallas_call(
        functools.partial(_norm_mod_kernel, shift_idx=shift_idx),
        grid=(t // tm,),
        in_specs=[
            pl.BlockSpec((tm, D_MODEL), lambda i: (i, 0)),
            pl.BlockSpec((1, 1, D_MODEL), lambda i: (layer, 0, 0)),
            pl.BlockSpec((1, 1, 6, D_MODEL), lambda i: (layer, mod_row(i, tm), 0, 0)),
        ],
        out_specs=pl.BlockSpec((tm, D_MODEL), lambda i: (i, 0)),
        out_shape=jax.ShapeDtypeStruct((t, D_MODEL), BF16),
        scratch_shapes=[pltpu.VMEM((8, D_MODEL), F32)],
        compiler_params=_cp("arbitrary"),
        name="norm_mod",
    )(x, norm_g, mod)


def _final_norm_kernel(x_ref, g_ref, o_ref):
    x = x_ref[...]
    y = x * lax.rsqrt(jnp.mean(x * x, axis=-1, keepdims=True) + EPS)
    o_ref[...] = y * g_ref[...]


def _final_norm(x, g):
    t = x.shape[0]
    tm = TM_NORM
    return pl.pallas_call(
        _final_norm_kernel,
        grid=(t // tm,),
        in_specs=[
            pl.BlockSpec((tm, D_MODEL), lambda i: (i, 0)),
            pl.BlockSpec((1, D_MODEL), lambda i: (0, 0)),
        ],
        out_specs=pl.BlockSpec((tm, D_MODEL), lambda i: (i, 0)),
        out_shape=jax.ShapeDtypeStruct((t, D_MODEL), F32),
        compiler_params=_cp("arbitrary"),
        name="final_norm",
    )(x, g.reshape(1, D_MODEL))


def _rope_tile(x, cos, sin_signed, first_half):
    swapped = jnp.where(first_half, pltpu.roll(x, LANES - 16, 1), pltpu.roll(x, 16, 1))
    return x * cos + swapped * sin_signed


def _mm_kernel(a_ref, w_ref, o_ref):
    w = w_ref[0]
    for r0, m in _row_chunks(a_ref.shape[0]):
        o_ref[r0:r0 + m, :] = _dot(a_ref[r0:r0 + m, :], w)


def _in_projection(h, w_in, layer, col0, ncols):
    t = h.shape[0]
    tm, tn = TM_PROJ, TN_PROJ
    j0 = col0 // tn
    return pl.pallas_call(
        _mm_kernel,
        grid=(t // tm, ncols // tn),
        in_specs=[
            pl.BlockSpec((tm, D_MODEL), lambda i, j: (i, 0)),
            pl.BlockSpec((1, D_MODEL, tn), lambda i, j: (layer, 0, j0 + j)),
        ],
        out_specs=pl.BlockSpec((tm, tn), lambda i, j: (i, j)),
        out_shape=jax.ShapeDtypeStruct((t, ncols), F32),
        compiler_params=_cp("arbitrary", "arbitrary"),
        name="in_projection",
    )(h, w_in)


def _rope_tables(n):
    rows = n // GRID_W
    row = jnp.repeat(jnp.arange(rows, dtype=F32), GRID_W)
    col = jnp.tile(jnp.arange(GRID_W, dtype=F32), rows)
    n_freq = DH_A // 4
    inv = ROPE_THETA ** (-jnp.arange(n_freq, dtype=F32) / n_freq)
    ang_r = row[:, None] * inv
    ang_c = col[:, None] * inv
    cos64 = jnp.concatenate([jnp.cos(ang_r), jnp.cos(ang_r), jnp.cos(ang_c), jnp.cos(ang_c)], -1)
    sin64 = jnp.concatenate([-jnp.sin(ang_r), jnp.sin(ang_r), -jnp.sin(ang_c), jnp.sin(ang_c)], -1)
    return jnp.tile(cos64, (1, 2)), jnp.tile(sin64, (1, 2))


def _mm_res_kernel(*refs, n_a, gate_idx):
    a_refs = refs[:n_a]
    w_ref, x_ref, mod_ref, o_ref = refs[n_a:]
    gate = mod_ref[0, 0, gate_idx:gate_idx + 1, :]
    for r0, m in _row_chunks(x_ref.shape[0]):
        rows = slice(r0, r0 + m)
        acc = None
        k0 = 0
        for a_ref in a_refs:
            kw = a_ref.shape[1]
            part = _dot(a_ref[rows, :], w_ref[0, k0:k0 + kw, :])
            acc = part if acc is None else acc + part
            k0 += kw
        o_ref[rows, :] = x_ref[rows, :] + gate * acc


def _residual_projection(a_parts, w, layer, x, mod, mod_row, gate_idx, tm, tn):
    t = x.shape[0]
    k, n = w.shape[1], w.shape[2]
    return pl.pallas_call(
        functools.partial(_mm_res_kernel, n_a=len(a_parts), gate_idx=gate_idx),
        grid=(t // tm, n // tn),
        in_specs=[pl.BlockSpec((tm, a.shape[1]), lambda i, j: (i, 0)) for a in a_parts] + [
            pl.BlockSpec((1, k, tn), lambda i, j: (layer, 0, j)),
            pl.BlockSpec((tm, tn), lambda i, j: (i, j)),
            pl.BlockSpec((1, 1, 6, tn), lambda i, j: (layer, mod_row(i, tm), 0, j)),
        ],
        out_specs=pl.BlockSpec((tm, tn), lambda i, j: (i, j)),
        out_shape=jax.ShapeDtypeStruct((t, n), F32),
        compiler_params=_cp("arbitrary", "arbitrary"),
        name="residual_projection",
    )(*a_parts, w, x, mod)


def _attn_kernel(*refs, lam_init, heads, sub, n_kv, rope):
    q_ref = refs[0]
    k_refs = refs[1:1 + n_kv]
    v_refs = refs[1 + n_kv:1 + 2 * n_kv]
    if rope:
        cos_ref, sin_ref, lam_ref, g_ref, o_ref = refs[1 + 2 * n_kv:]
    else:
        lam_ref, g_ref, o_ref = refs[1 + 2 * n_kv:]

    def rows_of(ref, cols):
        return ref[(0,) * (len(ref.shape) - 2) + (slice(None), cols)]

    def rotated(x, r0):
        m = x.shape[0]
        first_half = (lax.broadcasted_iota(jnp.int32, (m, LANES), 1) % 32) < 16
        return _rope_tile(x, cos_ref[r0:r0 + m, :], sin_ref[r0:r0 + m, :], first_half)

    lp = lam_ref[0]
    lam = (jnp.exp(jnp.sum(lp[0:1] * lp[1:2], axis=-1, keepdims=True))
           - jnp.exp(jnp.sum(lp[2:3] * lp[3:4], axis=-1, keepdims=True)) + lam_init)
    tq = q_ref.shape[1]
    nk = sum(r.shape[-2] for r in k_refs)
    gain = g_ref[0] * (1.0 - lam_init)
    edge = sub // 2 if tq > 2 * sub else sub
    bounds = [0] + list(range(edge, tq - edge + 1, sub)) + [tq]
    ones = jnp.ones((nk, LANES), BF16)
    for h in range(heads):
        cols = slice(h * LANES, (h + 1) * LANES)
        k_parts = [rows_of(r, cols) for r in k_refs]
        if rope:
            k_parts[-1] = rotated(k_parts[-1], 0)
        k = jnp.concatenate([kp.astype(BF16) for kp in k_parts], axis=0)
        v1 = jnp.concatenate([rows_of(r, cols).astype(BF16) for r in v_refs] , axis=0)
        v1 = jnp.concatenate([v1, ones], axis=1)
        for r0, r1 in zip(bounds[:-1], bounds[1:]):
            m = r1 - r0
            q = q_ref[0, r0:r1, cols]
            if rope:
                q = rotated(q, r0)
            q = (q * (DH_A ** -0.5)).astype(BF16)
            zero = jnp.zeros_like(q)
            lane = lax.broadcasted_iota(jnp.int32, (m, LANES), 1)
            qq = jnp.concatenate([jnp.where(lane < DH_A, q, zero), jnp.where(lane >= DH_A, q, zero)], axis=0)
            s = _dot_nt(qq, k)
            e = jnp.exp(s - jnp.max(s, axis=-1, keepdims=True)).astype(BF16)
            ov = _dot(e, v1)
            p = ov[:, :LANES] / ov[:, LANES:]
            o = p[:m] - lam * p[m:]
            o = o * lax.rsqrt(jnp.mean(o * o, axis=-1, keepdims=True) + EPS)
            o_ref[0, r0:r1, cols] = (o * gain).astype(o_ref.dtype)


def _diff_attention(p3, k_pieces, v_pieces, rope, attn_lambda, subln_g, layer, heads, sub):
    b, nq = p3.shape[0], p3.shape[1]
    lam_init = 0.8 - 0.6 * math.exp(-0.3 * layer)
    wd = heads * LANES
    specs = [pl.BlockSpec(shape, imap) for _, shape, imap in k_pieces + v_pieces]
    args = [arr for arr, _, _ in k_pieces + v_pieces]
    if rope is not None:
        specs += [pl.BlockSpec((nq, LANES), lambda bi, h: (0, 0))] * 2
        args += list(rope)
    return pl.pallas_call(
        functools.partial(_attn_kernel, lam_init=lam_init, heads=heads, sub=sub, n_kv=len(k_pieces),
                          rope=rope is not None),
        grid=(b, N_HEADS_A // heads),
        in_specs=[pl.BlockSpec((1, nq, wd), lambda bi, h: (bi, 0, h))] + specs + [
            pl.BlockSpec((1, 4, DH_A), lambda bi, h: (layer, 0, 0)),
            pl.BlockSpec((1, 1, DV_A), lambda bi, h: (layer, 0, 0)),
        ],
        out_specs=pl.BlockSpec((1, nq, wd), lambda bi, h: (bi, 0, h)),
        out_shape=jax.ShapeDtypeStruct((b, nq, WIDTH_A), BF16),
        compiler_params=_cp("arbitrary", "arbitrary"),
        name="diff_attention",
    )(p3, *args, attn_lambda, subln_g)


def _ret_kernel(ld_ref, q_ref, k_ref, v_ref, gf_ref, gb_ref, *rest, n, c, has_s0, layer):
    if has_s0:
        s0_ref, o_ref, sfin_ref, mask_ref, dec_ref, cdec_ref, state_ref, of_ref = rest
    else:
        o_ref, sfin_ref, mask_ref, dec_ref, cdec_ref, state_ref, of_ref = rest
    nc = n // c
    hq = QK_R

    lane_head = lax.broadcasted_iota(jnp.int32, (c, hq), 1) // DK_R
    row_head = lax.broadcasted_iota(jnp.int32, (hq, c), 0) // DK_R

    @pl.when(pl.program_id(0) == 0)
    def _():
        ri = lax.broadcasted_iota(jnp.int32, (c, c), 0)
        ci = lax.broadcasted_iota(jnp.int32, (c, c), 1)
        diff = (ri - ci).astype(F32)
        rowpos = lax.broadcasted_iota(jnp.int32, (c, hq), 0).astype(F32)
        for d in range(2):
            lg_lane = jnp.zeros((c, hq), F32)
            for h in range(N_HEADS_R):
                lg = ld_ref[layer, d, h]
                lg_lane = jnp.where(lane_head == h, lg, lg_lane)
                dist = diff if d == 0 else -diff
                mask_ref[d, h] = jnp.where(dist >= 0, jnp.exp(lg * jnp.maximum(dist, 0.0)), 0.0)
                cdec_ref[d, h] = jnp.exp(jnp.full((8, DV_R), lg * c, F32))
            if d == 0:
                dec_ref[0] = jnp.exp(lg_lane * (c - 1.0 - rowpos))
                dec_ref[1] = jnp.exp(lg_lane * (rowpos + 1.0))
            else:
                dec_ref[2] = jnp.exp(lg_lane * rowpos)
                dec_ref[3] = jnp.exp(lg_lane * (c - rowpos))

    rowh = lax.broadcasted_iota(jnp.int32, (hq, DV_R), 0) // DK_R
    for d in range(2):
        for h in range(N_HEADS_R):
            if has_s0:
                s0 = s0_ref[0, 0, d].reshape(hq, DV_R)
                state_ref[d, h] = jnp.where(rowh == h, s0, 0.0)
            else:
                state_ref[d, h] = jnp.zeros((hq, DV_R), F32)

    def chunk(ci_, d):
        start = pl.multiple_of(ci_ * c, c)
        rows = pl.ds(start, c)
        qc = q_ref[0, rows, :]
        kc = k_ref[0, rows, :] * (DK_R ** -0.5)
        kb = kc.astype(BF16)
        kdec_t = (kc * dec_ref[2 * d]).T.astype(BF16)
        qdec = (qc * dec_ref[2 * d + 1]).astype(BF16)
        qb = qc.astype(BF16)
        outs = []
        for h in range(N_HEADS_R):
            vh = v_ref[0, rows, h * DV_R:(h + 1) * DV_R].astype(BF16)
            s = _dot_nt(jnp.where(lane_head == h, qb, jnp.zeros_like(qb)), kb)
            o = _dot((s * mask_ref[d, h]).astype(BF16), vh)
            st = state_ref[d, h]
            o = o + _dot(jnp.where(lane_head == h, qdec, jnp.zeros_like(qdec)), st.astype(BF16))
            kv = _dot(jnp.where(row_head == h, kdec_t, jnp.zeros_like(kdec_t)), vh)
            state_ref[d, h] = st * cdec_ref[d, h, 0:1, :] + kv
            outs.append(o)
        return rows, outs

    def fwd_body(ci_, carry):
        rows, outs = chunk(ci_, 0)
        for h in range(N_HEADS_R):
            of_ref[rows, h * DV_R:(h + 1) * DV_R] = outs[h]
        return carry

    lax.fori_loop(0, nc, fwd_body, 0)

    def bwd_body(i, carry):
        rows, outs = chunk(nc - 1 - i, 1)
        for h in range(N_HEADS_R):
            sl = slice(h * DV_R, (h + 1) * DV_R)
            of = of_ref[rows, sl]
            ob = outs[h]
            yf = of * lax.rsqrt(jnp.mean(of * of, axis=-1, keepdims=True) + EPS)
            yb = ob * lax.rsqrt(jnp.mean(ob * ob, axis=-1, keepdims=True) + EPS)
            y = _silu(gf_ref[0, rows, sl]) * yf + _silu(gb_ref[0, rows, sl]) * yb
            o_ref[0, rows, sl] = y.astype(o_ref.dtype)
        return carry

    lax.fori_loop(0, nc, bwd_body, 0)

    for d in range(2):
        for h in range(N_HEADS_R):
            sfin_ref[0, d, h] = state_ref[d, h, h * DK_R:(h + 1) * DK_R, :]


def _retention(rest3, base, log_decay, layer, state_ret):
    b, n = rest3.shape[0], rest3.shape[1]
    c = RET_CHUNK
    has_s0 = state_ret is not None
    in_specs = [
        pl.BlockSpec(memory_space=pltpu.SMEM),
        pl.BlockSpec((1, n, QK_R), lambda bi: (bi, 0, (R_QR - base) // QK_R)),
        pl.BlockSpec((1, n, QK_R), lambda bi: (bi, 0, (R_KR - base) // QK_R)),
        pl.BlockSpec((1, n, WIDTH_R), lambda bi: (bi, 0, (R_VR - base) // WIDTH_R)),
        pl.BlockSpec((1, n, WIDTH_R), lambda bi: (bi, 0, (R_GF - base) // WIDTH_R)),
        pl.BlockSpec((1, n, WIDTH_R), lambda bi: (bi, 0, (R_GB - base) // WIDTH_R)),
    ]
    args = [log_decay, rest3, rest3, rest3, rest3, rest3]
    if has_s0:
        in_specs.append(pl.BlockSpec((1, 1, 2, N_HEADS_R, DK_R, DV_R), lambda bi: (bi, layer, 0, 0, 0, 0)))
        args.append(state_ret)
    return pl.pallas_call(
        functools.partial(_ret_kernel, n=n, c=c, has_s0=has_s0, layer=layer),
        grid=(b,),
        in_specs=in_specs,
        out_specs=[
            pl.BlockSpec((1, n, WIDTH_R), lambda bi: (bi, 0, 0)),
            pl.BlockSpec((1, 2, N_HEADS_R, DK_R, DV_R), lambda bi: (bi, 0, 0, 0, 0)),
        ],
        out_shape=[
            jax.ShapeDtypeStruct((b, n, WIDTH_R), BF16),
            jax.ShapeDtypeStruct((b, 2, N_HEADS_R, DK_R, DV_R), F32),
        ],
        scratch_shapes=[
            pltpu.VMEM((2, N_HEADS_R, c, c), F32),
            pltpu.VMEM((4, c, QK_R), F32),
            pltpu.VMEM((2, N_HEADS_R, 8, DV_R), F32),
            pltpu.VMEM((2, N_HEADS_R, QK_R, DV_R), F32),
            pltpu.VMEM((n, WIDTH_R), F32),
        ],
        compiler_params=_cp("arbitrary"),
        name="retention",
    )(*args)


def _pool_kernel(x_ref, w_ref, sc_ref, o_ref, *, n):
    t = lax.broadcasted_iota(jnp.int32, (n, POOL_GROUP_DIM), 0)

    def down(a, s):
        return jnp.where(t >= s, pltpu.roll(a, s, 0), 0.0)

    def up(a, s):
        return jnp.where(t < n - s, pltpu.roll(a, n - s, 0), 0.0)

    for g, win in enumerate(POOL_WINDOWS):
        sl = slice(g * POOL_GROUP_DIM, (g + 1) * POOL_GROUP_DIM)
        x = x_ref[0, :, sl]
        half = win // 2
        lead = x
        trail = down(x, 1)
        w = 1
        while w < half:
            lead = lead + up(lead, w)
            trail = trail + down(trail, w)
            w *= 2
        lo = jnp.clip(t - half, 0, n)
        hi = jnp.clip(t - half + win, 0, n)
        pooled = (lead + trail) / (hi - lo).astype(F32) - x
        y = _dot(pooled.astype(BF16), w_ref[0, g].astype(BF16))
        o_ref[0, :, sl] = (y * sc_ref[0, :, sl]).astype(o_ref.dtype)


def _pool_mixer(rest3, base, pool_w, pool_scale, layer):
    b, n = rest3.shape[0], rest3.shape[1]
    ng = len(POOL_WINDOWS)
    return pl.pallas_call(
        functools.partial(_pool_kernel, n=n),
        grid=(b,),
        in_specs=[
            pl.BlockSpec((1, n, WIDTH_P), lambda bi: (bi, 0, (R_POOL - base) // WIDTH_P)),
            pl.BlockSpec((1, ng, POOL_GROUP_DIM, POOL_GROUP_DIM), lambda bi: (layer, 0, 0, 0)),
            pl.BlockSpec((1, 1, WIDTH_P), lambda bi: (layer, 0, 0)),
        ],
        out_specs=pl.BlockSpec((1, n, WIDTH_P), lambda bi: (bi, 0, 0)),
        out_shape=jax.ShapeDtypeStruct((b, n, WIDTH_P), BF16),
        compiler_params=_cp("arbitrary"),
        name="pool_mixer",
    )(rest3, pool_w, pool_scale)


def _ffn_up_kernel(h_ref, wg_ref, wv_ref, cw_ref, cb_ref, o_ref, *, seq):
    wg = wg_ref[0]
    wv = wv_ref[0]
    chunks = _row_chunks(h_ref.shape[0])
    g = jnp.concatenate([_dot(h_ref[r:r + m, :], wg) for r, m in chunks], axis=0)
    v = jnp.concatenate([_dot(h_ref[r:r + m, :], wv) for r, m in chunks], axis=0)
    tm = g.shape[0]
    pos = lax.broadcasted_iota(jnp.int32, g.shape, 0) % seq
    g_prev = jnp.where(pos == 0, 0.0, pltpu.roll(g, 1, 0))
    g_next = jnp.where(pos == seq - 1, 0.0, pltpu.roll(g, tm - 1, 0))
    gc = g_prev * cw_ref[0, 0:1, :] + g * cw_ref[0, 1:2, :] + g_next * cw_ref[0, 2:3, :] + cb_ref[0]
    o_ref[...] = (_silu(gc) * v).astype(o_ref.dtype)


def _ffn_up(h, w_up, layer, conv_w, conv_b, seq):
    t = h.shape[0]
    tm, tn = TM_FFN, TN_FFN
    nj = D_FF // tn
    return pl.pallas_call(
        functools.partial(_ffn_up_kernel, seq=seq),
        grid=(t // tm, nj),
        in_specs=[
            pl.BlockSpec((tm, D_MODEL), lambda i, j: (i, 0)),
            pl.BlockSpec((1, D_MODEL, tn), lambda i, j: (layer, 0, j)),
            pl.BlockSpec((1, D_MODEL, tn), lambda i, j: (layer, 0, j + nj)),
            pl.BlockSpec((1, 3, tn), lambda i, j: (layer, 0, j)),
            pl.BlockSpec((1, 1, tn), lambda i, j: (layer, 0, j)),
        ],
        out_specs=pl.BlockSpec((tm, tn), lambda i, j: (i, j)),
        out_shape=jax.ShapeDtypeStruct((t, D_FF), BF16),
        compiler_params=_cp("arbitrary", "arbitrary"),
        name="ffn_up_convglu",
    )(h, w_up, w_up, conv_w, conv_b)


def _trunk_layer(x, layer, b, n, mod, mod_row, rope, ctx, wts):
    (norm1_g, w_in, attn_lambda, subln_g, log_decay, pool_w, pool_scale, w_out, norm2_g, w_up, conv_w,
     conv_b, w_down) = wts
    h = _norm_mod(x, norm1_g, layer, mod, mod_row, SHIFT1)
    if rope is None:
        q3, k3, v3 = (_in_projection(h, w_in, layer, c0, WIDTH_A).reshape(b, n, WIDTH_A)
                      for c0 in (C_QA, C_KA, C_VA))
        rest_base = R_QR
        rest3 = _in_projection(h, w_in, layer, rest_base, IN_WIDTH - rest_base).reshape(b, n, -1)
        heads, state_ret = N_HEADS_A, None
        k_pieces = [(k3, (1, n, WIDTH_A), lambda bi, h: (bi, 0, 0))]
        v_pieces = [(v3, (1, n, WIDTH_A), lambda bi, h: (bi, 0, 0))]
    else:
        cache_k, cache_v, state_ret = ctx
        past = cache_k.shape[2]
        heads = ATT_HEADS_LATENT
        wd = heads * LANES
        q3 = rest3 = _in_projection(h, w_in, layer, 0, IN_WIDTH).reshape(b, n, IN_WIDTH)
        rest_base = 0
        k3 = v3 = None
        k_pieces = [(cache_k, (1, 1, past, wd), lambda bi, h: (bi, layer, 0, h)),
                    (q3, (1, n, wd), lambda bi, h: (bi, 0, C_KA // wd + h))]
        v_pieces = [(cache_v, (1, 1, past, wd), lambda bi, h: (bi, layer, 0, h)),
                    (q3, (1, n, wd), lambda bi, h: (bi, 0, C_VA // wd + h))]
    oa = _diff_attention(q3, k_pieces, v_pieces, rope, attn_lambda, subln_g, layer, heads, min(ATT_SUB, n))
    orr, s_fin = _retention(rest3, rest_base, log_decay, layer, state_ret)
    op = _pool_mixer(rest3, rest_base, pool_w, pool_scale, layer)
    parts = [oa.reshape(b * n, WIDTH_A), orr.reshape(b * n, WIDTH_R), op.reshape(b * n, WIDTH_P)]
    x = _residual_projection(parts, w_out, layer, x, mod, mod_row, GATE1, TM_PROJ, TN_PROJ)
    h2 = _norm_mod(x, norm2_g, layer, mod, mod_row, SHIFT2)
    a = _ffn_up(h2, w_up, layer, conv_w, conv_b, n)
    x = _residual_projection([a], w_down, layer, x, mod, mod_row, GATE2, TM_DOWN, TN_DOWN)
    return x, k3, v3, s_fin


def kernel(x_prompt, x_sample, cache_k, cache_v, state_ret, c, c_ctx, ada_w, ada_b, norm1_g, w_in,
           attn_lambda, attn_subln_g, ret_log_decay, pool_w, pool_scale, w_out, norm2_g,
           ffn_w_up, ffn_conv_w, ffn_conv_b, ffn_w_down, final_g):
    bp, sp, _ = x_prompt.shape
    bs, ss, _ = x_sample.shape
    past = cache_k.shape[2]
    ctx_row = bs
    assert bs < MOD_ROWS and x_prompt.shape[2] == D_MODEL and x_sample.shape[2] == D_MODEL
    for seq, tokens in ((sp, bp * sp), (ss, bs * ss)):
        assert TM_FFN % seq == 0 and tokens % TM_FFN == 0 and tokens % TM_PROJ == 0
        assert seq % RET_CHUNK == 0 and seq % min(ATT_SUB, seq) == 0 and seq % GRID_W == 0
    assert ss % TM_PROJ == 0 and ss % TM_DOWN == 0 and ss % TM_NORM == 0

    cs = jnp.concatenate([c, c_ctx[None, :], jnp.zeros((MOD_ROWS - bs - 1, D_MODEL), F32)], axis=0)
    mod = _modulation(cs, ada_w, ada_b).reshape(DEPTH, MOD_ROWS, 6, D_MODEL)

    rope = _rope_tables(ss)
    ck = cache_k.reshape(bs, DEPTH, past, QK_A)
    cv = cache_v.reshape(bs, DEPTH, past, WIDTH_A)

    w_in_b = w_in.astype(BF16)
    w_out_b = w_out.astype(BF16)
    w_up_b = ffn_w_up.astype(BF16)
    w_down_b = ffn_w_down.astype(BF16)
    norm1_3 = norm1_g.reshape(DEPTH, 1, D_MODEL)
    norm2_3 = norm2_g.reshape(DEPTH, 1, D_MODEL)
    wts = (norm1_3, w_in_b, attn_lambda, attn_subln_g.reshape(DEPTH, 1, DV_A), ret_log_decay, pool_w,
           pool_scale.reshape(DEPTH, 1, WIDTH_P), w_out_b, norm2_3, w_up_b, ffn_conv_w,
           ffn_conv_b.reshape(DEPTH, 1, D_FF), w_down_b)

    def prompt_row(i, tm):
        return ctx_row

    def sample_row(i, tm):
        return (i * tm) // ss

    xp = x_prompt.reshape(bp * sp, D_MODEL)
    xs = x_sample.reshape(bs * ss, D_MODEL)
    new_k, new_v, new_s = [], [], []
    for l in range(DEPTH):
        xp, k_l, v_l, s_l = _trunk_layer(xp, l, bp, sp, mod, prompt_row, None, None, wts)
        new_k.append(k_l.reshape(bp, sp, 2 * N_HEADS_A, DH_A))
        new_v.append(v_l.reshape(bp, sp, N_HEADS_A, DV_A))
        new_s.append(s_l)
        xs, _, _, _ = _trunk_layer(xs, l, bs, ss, mod, sample_row, rope, (ck, cv, state_ret), wts)
    y_prompt = _final_norm(xp, final_g).reshape(bp, sp, D_MODEL)
    y_sample = _final_norm(xs, final_g).reshape(bs, ss, D_MODEL)
    return (y_prompt, y_sample, jnp.stack(new_k, axis=1), jnp.stack(new_v, axis=1), jnp.stack(new_s, axis=1))
```

```python
import functools
import math

import jax
import jax.numpy as jnp
from jax import lax
from jax.experimental import pallas as pl
from jax.experimental.pallas import tpu as pltpu

F32 = jnp.float32
BF16 = jnp.bfloat16

D_MODEL = 2048
DEPTH = 2
GRID_W = 64
N_HEADS_A = 8
DH_A = 64
DV_A = 128
WIDTH_A = 1024
QK_A = 1024
N_HEADS_R = 4
DK_R = 64
DV_R = 128
WIDTH_R = 512
QK_R = 256
POOL_WINDOWS = (2, 4, 8, 16)
POOL_GROUP_DIM = 128
WIDTH_P = 512
IN_WIDTH = 5632
D_FF = 5632
ROPE_THETA = 10000.0
EPS = 1e-6

C_QA, C_KA, C_VA = 0, 1024, 2048
R_QR, R_KR, R_VR, R_GF, R_GB, R_POOL = 3072, 3328, 3584, 4096, 4608, 5120

SHIFT1, SCALE1, GATE1, SHIFT2, SCALE2, GATE2 = range(6)
MOD_ROWS = 16
LANES = 128
VMEM_LIMIT = 52 * 1024 * 1024
DOT_ROWS = 1024

TM_PROJ, TN_PROJ = 2048, 512
TM_DOWN, TN_DOWN = 1024, 512
TM_FFN, TN_FFN = 2048, 512
TM_NORM = 1024
NORM_ROWS = 16
RET_CHUNK = 256
ATT_SUB = 256
ATT_HEADS_LATENT = 2


def _cp(*sem):
    return pltpu.CompilerParams(dimension_semantics=sem, vmem_limit_bytes=VMEM_LIMIT)


def _silu(x):
    return x * (1.0 / (1.0 + jnp.exp(-x)))


def _dot_nt(a, b):
    return lax.dot_general(a, b, (((1,), (1,)), ((), ())), preferred_element_type=F32)


def _dot(a, b):
    return jnp.dot(a, b, preferred_element_type=F32)


def _row_chunks(tm):
    return [(r, min(DOT_ROWS, tm - r)) for r in range(0, tm, DOT_ROWS)]


def _mod_kernel(c_ref, w_ref, b_ref, o_ref):
    a = _silu(c_ref[...]).astype(BF16)
    o_ref[0] = _dot(a, w_ref[0].astype(BF16)) + b_ref[0]


def _modulation(cs, ada_w, ada_b):
    tn = 1024
    n = ada_w.shape[-1]
    return pl.pallas_call(
        _mod_kernel,
        grid=(DEPTH, n // tn),
        in_specs=[
            pl.BlockSpec((MOD_ROWS, D_MODEL), lambda l, j: (0, 0)),
            pl.BlockSpec((1, D_MODEL, tn), lambda l, j: (l, 0, j)),
            pl.BlockSpec((1, 1, tn), lambda l, j: (l, 0, j)),
        ],
        out_specs=pl.BlockSpec((1, MOD_ROWS, tn), lambda l, j: (l, 0, j)),
        out_shape=jax.ShapeDtypeStruct((DEPTH, MOD_ROWS, n), F32),
        compiler_params=_cp("arbitrary", "arbitrary"),
        name="modulation",
    )(cs, ada_w, ada_b.reshape(DEPTH, 1, n))


def _norm_mod_rows(x_ref, g_ref, mod_ref, h_ref, gain_ref, shift_idx):
    gain_ref[0:1, :] = g_ref[0] * (1.0 + mod_ref[0, 0, shift_idx + 1:shift_idx + 2, :])
    gain_ref[1:2, :] = mod_ref[0, 0, shift_idx:shift_idx + 1, :]

    def body(c, carry):
        rows = pl.ds(pl.multiple_of(c * NORM_ROWS, NORM_ROWS), NORM_ROWS)
        x = x_ref[rows, :]
        r = lax.rsqrt(jnp.mean(x * x, axis=-1, keepdims=True) + EPS)
        h_ref[rows, :] = ((x * r) * gain_ref[0:1, :] + gain_ref[1:2, :]).astype(h_ref.dtype)
        return carry

    lax.fori_loop(0, x_ref.shape[0] // NORM_ROWS, body, 0, unroll=4)


def _norm_in_specs(tm, layer, mod_row):
    return [
        pl.BlockSpec((tm, D_MODEL), lambda i, j: (i, 0), pipeline_mode=pl.Buffered(1)),
        pl.BlockSpec((1, 1, D_MODEL), lambda i, j: (layer, 0, 0)),
        pl.BlockSpec((1, 1, 6, D_MODEL), lambda i, j: (layer, mod_row(i, tm), 0, 0)),
    ]


def _norm_scratch(tm):
    return [pltpu.VMEM((tm, D_MODEL), BF16),
            pltpu.VMEM((8, D_MODEL), F32)]


def _final_norm_kernel(x_ref, g_ref, o_ref):
    x = x_ref[...]
    y = x * lax.rsqrt(jnp.mean(x * x, axis=-1, keepdims=True) + EPS)
    o_ref[...] = y * g_ref[...]


def _final_norm(x, g):
    t = x.shape[0]
    tm = TM_NORM
    return pl.pallas_call(
        _final_norm_kernel,
        grid=(t // tm,),
        in_specs=[
            pl.BlockSpec((tm, D_MODEL), lambda i: (i, 0)),
            pl.BlockSpec((1, D_MODEL), lambda i: (0, 0)),
        ],
        out_specs=pl.BlockSpec((tm, D_MODEL), lambda i: (i, 0)),
        out_shape=jax.ShapeDtypeStruct((t, D_MODEL), F32),
        compiler_params=_cp("arbitrary"),
        name="final_norm",
    )(x, g.reshape(1, D_MODEL))


def _rope_tile(x, cos, sin_signed, first_half):
    swapped = jnp.where(first_half, pltpu.roll(x, LANES - 16, 1), pltpu.roll(x, 16, 1))
    return x * cos + swapped * sin_signed


def _in_proj_kernel(x_ref, g_ref, mod_ref, w_ref, o_ref, h_ref, gain_ref):
    @pl.when(pl.program_id(1) == 0)
    def _():
        _norm_mod_rows(x_ref, g_ref, mod_ref, h_ref, gain_ref, SHIFT1)

    w = w_ref[0]
    for r0, m in _row_chunks(h_ref.shape[0]):
        o_ref[r0:r0 + m, :] = _dot(h_ref[r0:r0 + m, :], w)


def _in_projection(x, norm_g, mod, mod_row, w_in, layer):
    t = x.shape[0]
    tm, tn = TM_PROJ, TN_PROJ
    return pl.pallas_call(
        _in_proj_kernel,
        grid=(t // tm, IN_WIDTH // tn),
        in_specs=_norm_in_specs(tm, layer, mod_row) + [
            pl.BlockSpec((1, D_MODEL, tn), lambda i, j: (layer, 0, j)),
        ],
        out_specs=pl.BlockSpec((tm, tn), lambda i, j: (i, j)),
        out_shape=jax.ShapeDtypeStruct((t, IN_WIDTH), F32),
        scratch_shapes=_norm_scratch(tm),
        compiler_params=_cp("arbitrary", "arbitrary"),
        name="in_projection",
    )(x, norm_g, mod, w_in)


def _rope_tables(n):
    rows = n // GRID_W
    row = jnp.repeat(jnp.arange(rows, dtype=F32), GRID_W)
    col = jnp.tile(jnp.arange(GRID_W, dtype=F32), rows)
    n_freq = DH_A // 4
    inv = ROPE_THETA ** (-jnp.arange(n_freq, dtype=F32) / n_freq)
    ang_r = row[:, None] * inv
    ang_c = col[:, None] * inv
    cos64 = jnp.concatenate([jnp.cos(ang_r), jnp.cos(ang_r), jnp.cos(ang_c), jnp.cos(ang_c)], -1)
    sin64 = jnp.concatenate([-jnp.sin(ang_r), jnp.sin(ang_r), -jnp.sin(ang_c), jnp.sin(ang_c)], -1)
    return jnp.tile(cos64, (1, 2)), jnp.tile(sin64, (1, 2))


def _mm_res_kernel(*refs, n_a, gate_idx):
    a_refs = refs[:n_a]
    w_ref, x_ref, mod_ref, o_ref = refs[n_a:]
    gate = mod_ref[0, 0, gate_idx:gate_idx + 1, :]
    for r0, m in _row_chunks(x_ref.shape[0]):
        rows = slice(r0, r0 + m)
        acc = None
        k0 = 0
        for a_ref in a_refs:
            kw = a_ref.shape[1]
            part = _dot(a_ref[rows, :], w_ref[0, k0:k0 + kw, :])
            acc = part if acc is None else acc + part
            k0 += kw
        o_ref[rows, :] = x_ref[rows, :] + gate * acc


def _residual_projection(a_parts, w, layer, x, mod, mod_row, gate_idx, tm, tn):
    t = x.shape[0]
    k, n = w.shape[1], w.shape[2]
    return pl.pallas_call(
        functools.partial(_mm_res_kernel, n_a=len(a_parts), gate_idx=gate_idx),
        grid=(t // tm, n // tn),
        in_specs=[pl.BlockSpec((tm, a.shape[1]), lambda i, j: (i, 0)) for a in a_parts] + [
            pl.BlockSpec((1, k, tn), lambda i, j: (layer, 0, j)),
            pl.BlockSpec((tm, tn), lambda i, j: (i, j)),
            pl.BlockSpec((1, 1, 6, tn), lambda i, j: (layer, mod_row(i, tm), 0, j)),
        ],
        out_specs=pl.BlockSpec((tm, tn), lambda i, j: (i, j)),
        out_shape=jax.ShapeDtypeStruct((t, n), F32),
        compiler_params=_cp("arbitrary", "arbitrary"),
        name="residual_projection",
    )(*a_parts, w, x, mod)


def _attn_kernel(*refs, lam_init, heads, sub, n_kv, rope):
    q_ref = refs[0]
    k_refs = refs[1:1 + n_kv]
    v_refs = refs[1 + n_kv:1 + 2 * n_kv]
    if rope:
        cos_ref, sin_ref, lam_ref, g_ref, o_ref = refs[1 + 2 * n_kv:]
    else:
        lam_ref, g_ref, o_ref = refs[1 + 2 * n_kv:]

    def rows_of(ref, cols):
        return ref[(0,) * (len(ref.shape) - 2) + (slice(None), cols)]

    def rotated(x, r0):
        m = x.shape[0]
        first_half = (lax.broadcasted_iota(jnp.int32, (m, LANES), 1) % 32) < 16
        return _rope_tile(x, cos_ref[r0:r0 + m, :], sin_ref[r0:r0 + m, :], first_half)

    lp = lam_ref[0]
    lam = (jnp.exp(jnp.sum(lp[0:1] * lp[1:2], axis=-1, keepdims=True))
           - jnp.exp(jnp.sum(lp[2:3] * lp[3:4], axis=-1, keepdims=True)) + lam_init)
    tq = q_ref.shape[1]
    nk = sum(r.shape[-2] for r in k_refs)
    gain = g_ref[0] * (1.0 - lam_init)
    edge = sub // 2 if tq > 2 * sub else sub
    bounds = [0] + list(range(edge, tq - edge + 1, sub)) + [tq]
    ones = jnp.ones((nk, LANES), BF16)
    for h in range(heads):
        cols = slice(h * LANES, (h + 1) * LANES)
        k_parts = [rows_of(r, cols) for r in k_refs]
        if rope:
            k_parts[-1] = rotated(k_parts[-1], 0)
        k = jnp.concatenate([kp.astype(BF16) for kp in k_parts], axis=0)
        v1 = jnp.concatenate([rows_of(r, cols).astype(BF16) for r in v_refs] , axis=0)
        v1 = jnp.concatenate([v1, ones], axis=1)
        for r0, r1 in zip(bounds[:-1], bounds[1:]):
            m = r1 - r0
            q = q_ref[0, r0:r1, cols]
            if rope:
                q = rotated(q, r0)
            q = (q * (DH_A ** -0.5)).astype(BF16)
            zero = jnp.zeros_like(q)
            lane = lax.broadcasted_iota(jnp.int32, (m, LANES), 1)
            qq = jnp.concatenate([jnp.where(lane < DH_A, q, zero), jnp.where(lane >= DH_A, q, zero)], axis=0)
            s = _dot_nt(qq, k)
            e = jnp.exp(s - jnp.max(s, axis=-1, keepdims=True)).astype(BF16)
            ov = _dot(e, v1)
            p = ov[:, :LANES] / ov[:, LANES:]
            o = p[:m] - lam * p[m:]
            o = o * lax.rsqrt(jnp.mean(o * o, axis=-1, keepdims=True) + EPS)
            o_ref[0, r0:r1, cols] = (o * gain).astype(o_ref.dtype)


def _diff_attention(p3, k_pieces, v_pieces, rope, attn_lambda, subln_g, layer, heads, sub):
    b, nq = p3.shape[0], p3.shape[1]
    lam_init = 0.8 - 0.6 * math.exp(-0.3 * layer)
    wd = heads * LANES
    specs = [pl.BlockSpec(shape, imap) for _, shape, imap in k_pieces + v_pieces]
    args = [arr for arr, _, _ in k_pieces + v_pieces]
    if rope is not None:
        specs += [pl.BlockSpec((nq, LANES), lambda bi, h: (0, 0))] * 2
        args += list(rope)
    return pl.pallas_call(
        functools.partial(_attn_kernel, lam_init=lam_init, heads=heads, sub=sub, n_kv=len(k_pieces),
                          rope=rope is not None),
        grid=(b, N_HEADS_A // heads),
        in_specs=[pl.BlockSpec((1, nq, wd), lambda bi, h: (bi, 0, h))] + specs + [
            pl.BlockSpec((1, 4, DH_A), lambda bi, h: (layer, 0, 0)),
            pl.BlockSpec((1, 1, DV_A), lambda bi, h: (layer, 0, 0)),
        ],
        out_specs=pl.BlockSpec((1, nq, wd), lambda bi, h: (bi, 0, h)),
        out_shape=jax.ShapeDtypeStruct((b, nq, WIDTH_A), BF16),
        compiler_params=_cp("arbitrary", "arbitrary"),
        name="diff_attention",
    )(p3, *args, attn_lambda, subln_g)


def _ret_kernel(ld_ref, q_ref, k_ref, v_ref, gf_ref, gb_ref, *rest, n, c, has_s0, layer):
    if has_s0:
        s0_ref, o_ref, sfin_ref, mask_ref, dec_ref, cdec_ref, state_ref, of_ref = rest
    else:
        o_ref, sfin_ref, mask_ref, dec_ref, cdec_ref, state_ref, of_ref = rest
    nc = n // c
    hq = QK_R

    lane_head = lax.broadcasted_iota(jnp.int32, (c, hq), 1) // DK_R
    row_head = lax.broadcasted_iota(jnp.int32, (hq, c), 0) // DK_R

    @pl.when(pl.program_id(0) == 0)
    def _():
        ri = lax.broadcasted_iota(jnp.int32, (c, c), 0)
        ci = lax.broadcasted_iota(jnp.int32, (c, c), 1)
        diff = (ri - ci).astype(F32)
        rowpos = lax.broadcasted_iota(jnp.int32, (c, hq), 0).astype(F32)
        for d in range(2):
            lg_lane = jnp.zeros((c, hq), F32)
            for h in range(N_HEADS_R):
                lg = ld_ref[layer, d, h]
                lg_lane = jnp.where(lane_head == h, lg, lg_lane)
                dist = diff if d == 0 else -diff
                mask_ref[d, h] = jnp.where(dist >= 0, jnp.exp(lg * jnp.maximum(dist, 0.0)), 0.0)
                cdec_ref[d, h] = jnp.exp(jnp.full((8, DV_R), lg * c, F32))
            if d == 0:
                dec_ref[0] = jnp.exp(lg_lane * (c - 1.0 - rowpos))
                dec_ref[1] = jnp.exp(lg_lane * (rowpos + 1.0))
            else:
                dec_ref[2] = jnp.exp(lg_lane * rowpos)
                dec_ref[3] = jnp.exp(lg_lane * (c - rowpos))

    rowh = lax.broadcasted_iota(jnp.int32, (hq, DV_R), 0) // DK_R
    for d in range(2):
        for h in range(N_HEADS_R):
            if has_s0:
                s0 = s0_ref[0, 0, d].reshape(hq, DV_R)
                state_ref[d, h] = jnp.where(rowh == h, s0, 0.0)
            else:
                state_ref[d, h] = jnp.zeros((hq, DV_R), F32)

    def chunk(ci_, d):
        start = pl.multiple_of(ci_ * c, c)
        rows = pl.ds(start, c)
        qc = q_ref[0, rows, :]
        kc = k_ref[0, rows, :] * (DK_R ** -0.5)
        kb = kc.astype(BF16)
        kdec_t = (kc * dec_ref[2 * d]).T.astype(BF16)
        qdec = (qc * dec_ref[2 * d + 1]).astype(BF16)
        qb = qc.astype(BF16)
        outs = []
        for h in range(N_HEADS_R):
            vh = v_ref[0, rows, h * DV_R:(h + 1) * DV_R].astype(BF16)
            s = _dot_nt(jnp.where(lane_head == h, qb, jnp.zeros_like(qb)), kb)
            o = _dot((s * mask_ref[d, h]).astype(BF16), vh)
            st = state_ref[d, h]
            o = o + _dot(jnp.where(lane_head == h, qdec, jnp.zeros_like(qdec)), st.astype(BF16))
            kv = _dot(jnp.where(row_head == h, kdec_t, jnp.zeros_like(kdec_t)), vh)
            state_ref[d, h] = st * cdec_ref[d, h, 0:1, :] + kv
            outs.append(o)
        return rows, outs

    def fwd_body(ci_, carry):
        rows, outs = chunk(ci_, 0)
        for h in range(N_HEADS_R):
            of_ref[rows, h * DV_R:(h + 1) * DV_R] = outs[h]
        return carry

    lax.fori_loop(0, nc, fwd_body, 0)

    def bwd_body(i, carry):
        rows, outs = chunk(nc - 1 - i, 1)
        for h in range(N_HEADS_R):
            sl = slice(h * DV_R, (h + 1) * DV_R)
            of = of_ref[rows, sl]
            ob = outs[h]
            yf = of * lax.rsqrt(jnp.mean(of * of, axis=-1, keepdims=True) + EPS)
            yb = ob * lax.rsqrt(jnp.mean(ob * ob, axis=-1, keepdims=True) + EPS)
            y = _silu(gf_ref[0, rows, sl]) * yf + _silu(gb_ref[0, rows, sl]) * yb
            o_ref[0, rows, sl] = y.astype(o_ref.dtype)
        return carry

    lax.fori_loop(0, nc, bwd_body, 0)

    for d in range(2):
        for h in range(N_HEADS_R):
            sfin_ref[0, d, h] = state_ref[d, h, h * DK_R:(h + 1) * DK_R, :]


def _retention(rest3, log_decay, layer, state_ret):
    b, n = rest3.shape[0], rest3.shape[1]
    c = RET_CHUNK
    has_s0 = state_ret is not None
    in_specs = [
        pl.BlockSpec(memory_space=pltpu.SMEM),
        pl.BlockSpec((1, n, QK_R), lambda bi: (bi, 0, R_QR // QK_R)),
        pl.BlockSpec((1, n, QK_R), lambda bi: (bi, 0, R_KR // QK_R)),
        pl.BlockSpec((1, n, WIDTH_R), lambda bi: (bi, 0, R_VR // WIDTH_R)),
        pl.BlockSpec((1, n, WIDTH_R), lambda bi: (bi, 0, R_GF // WIDTH_R)),
        pl.BlockSpec((1, n, WIDTH_R), lambda bi: (bi, 0, R_GB // WIDTH_R)),
    ]
    args = [log_decay, rest3, rest3, rest3, rest3, rest3]
    if has_s0:
        in_specs.append(pl.BlockSpec((1, 1, 2, N_HEADS_R, DK_R, DV_R), lambda bi: (bi, layer, 0, 0, 0, 0)))
        args.append(state_ret)
    return pl.pallas_call(
        functools.partial(_ret_kernel, n=n, c=c, has_s0=has_s0, layer=layer),
        grid=(b,),
        in_specs=in_specs,
        out_specs=[
            pl.BlockSpec((1, n, WIDTH_R), lambda bi: (bi, 0, 0)),
            pl.BlockSpec((1, 2, N_HEADS_R, DK_R, DV_R), lambda bi: (bi, 0, 0, 0, 0)),
        ],
        out_shape=[
            jax.ShapeDtypeStruct((b, n, WIDTH_R), BF16),
            jax.ShapeDtypeStruct((b, 2, N_HEADS_R, DK_R, DV_R), F32),
        ],
        scratch_shapes=[
            pltpu.VMEM((2, N_HEADS_R, c, c), F32),
            pltpu.VMEM((4, c, QK_R), F32),
            pltpu.VMEM((2, N_HEADS_R, 8, DV_R), F32),
            pltpu.VMEM((2, N_HEADS_R, QK_R, DV_R), F32),
            pltpu.VMEM((n, WIDTH_R), F32),
        ],
        compiler_params=_cp("arbitrary"),
        name="retention",
    )(*args)


def _pool_kernel(x_ref, w_ref, sc_ref, o_ref, *, n):
    t = lax.broadcasted_iota(jnp.int32, (n, POOL_GROUP_DIM), 0)

    def down(a, s):
        return jnp.where(t >= s, pltpu.roll(a, s, 0), 0.0)

    def up(a, s):
        return jnp.where(t < n - s, pltpu.roll(a, n - s, 0), 0.0)

    for g, win in enumerate(POOL_WINDOWS):
        sl = slice(g * POOL_GROUP_DIM, (g + 1) * POOL_GROUP_DIM)
        x = x_ref[0, :, sl]
        half = win // 2
        lead = x
        trail = down(x, 1)
        w = 1
        while w < half:
            lead = lead + up(lead, w)
            trail = trail + down(trail, w)
            w *= 2
        lo = jnp.clip(t - half, 0, n)
        hi = jnp.clip(t - half + win, 0, n)
        pooled = (lead + trail) / (hi - lo).astype(F32) - x
        y = _dot(pooled.astype(BF16), w_ref[0, g].astype(BF16))
        o_ref[0, :, sl] = (y * sc_ref[0, :, sl]).astype(o_ref.dtype)


def _pool_mixer(rest3, pool_w, pool_scale, layer):
    b, n = rest3.shape[0], rest3.shape[1]
    ng = len(POOL_WINDOWS)
    return pl.pallas_call(
        functools.partial(_pool_kernel, n=n),
        grid=(b,),
        in_specs=[
            pl.BlockSpec((1, n, WIDTH_P), lambda bi: (bi, 0, R_POOL // WIDTH_P)),
            pl.BlockSpec((1, ng, POOL_GROUP_DIM, POOL_GROUP_DIM), lambda bi: (layer, 0, 0, 0)),
            pl.BlockSpec((1, 1, WIDTH_P), lambda bi: (layer, 0, 0)),
        ],
        out_specs=pl.BlockSpec((1, n, WIDTH_P), lambda bi: (bi, 0, 0)),
        out_shape=jax.ShapeDtypeStruct((b, n, WIDTH_P), BF16),
        compiler_params=_cp("arbitrary"),
        name="pool_mixer",
    )(rest3, pool_w, pool_scale)


def _ffn_up_kernel(x_ref, ng_ref, mod_ref, wg_ref, wv_ref, cw_ref, cb_ref, o_ref, h_ref, gain_ref, *, seq):
    @pl.when(pl.program_id(1) == 0)
    def _():
        _norm_mod_rows(x_ref, ng_ref, mod_ref, h_ref, gain_ref, SHIFT2)

    wg = wg_ref[0]
    wv = wv_ref[0]
    chunks = _row_chunks(h_ref.shape[0])
    g = jnp.concatenate([_dot(h_ref[r:r + m, :], wg) for r, m in chunks], axis=0)
    v = jnp.concatenate([_dot(h_ref[r:r + m, :], wv) for r, m in chunks], axis=0)
    tm = g.shape[0]
    pos = lax.broadcasted_iota(jnp.int32, g.shape, 0) % seq
    g_prev = jnp.where(pos == 0, 0.0, pltpu.roll(g, 1, 0))
    g_next = jnp.where(pos == seq - 1, 0.0, pltpu.roll(g, tm - 1, 0))
    gc = g_prev * cw_ref[0, 0:1, :] + g * cw_ref[0, 1:2, :] + g_next * cw_ref[0, 2:3, :] + cb_ref[0]
    o_ref[...] = (_silu(gc) * v).astype(o_ref.dtype)


def _ffn_up(x, norm_g, mod, mod_row, w_up, layer, conv_w, conv_b, seq):
    t = x.shape[0]
    tm, tn = TM_FFN, TN_FFN
    nj = D_FF // tn
    return pl.pallas_call(
        functools.partial(_ffn_up_kernel, seq=seq),
        grid=(t // tm, nj),
        in_specs=_norm_in_specs(tm, layer, mod_row) + [
            pl.BlockSpec((1, D_MODEL, tn), lambda i, j: (layer, 0, j)),
            pl.BlockSpec((1, D_MODEL, tn), lambda i, j: (layer, 0, j + nj)),
            pl.BlockSpec((1, 3, tn), lambda i, j: (layer, 0, j)),
            pl.BlockSpec((1, 1, tn), lambda i, j: (layer, 0, j)),
        ],
        out_specs=pl.BlockSpec((tm, tn), lambda i, j: (i, j)),
        out_shape=jax.ShapeDtypeStruct((t, D_FF), BF16),
        scratch_shapes=_norm_scratch(tm),
        compiler_params=_cp("arbitrary", "arbitrary"),
        name="ffn_up_convglu",
    )(x, norm_g, mod, w_up, w_up, conv_w, conv_b)


def _trunk_layer(x, layer, b, n, mod, mod_row, rope, ctx, wts):
    (norm1_g, w_in, attn_lambda, subln_g, log_decay, pool_w, pool_scale, w_out, norm2_g, w_up, conv_w,
     conv_b, w_down) = wts
    p3 = _in_projection(x, norm1_g, mod, mod_row, w_in, layer).reshape(b, n, IN_WIDTH)
    if rope is None:
        heads, state_ret = N_HEADS_A, None
        wd = heads * LANES
        k_pieces = [(p3, (1, n, wd), lambda bi, h: (bi, 0, C_KA // wd + h))]
        v_pieces = [(p3, (1, n, wd), lambda bi, h: (bi, 0, C_VA // wd + h))]
    else:
        cache_k, cache_v, state_ret = ctx
        past = cache_k.shape[2]
        heads = ATT_HEADS_LATENT
        wd = heads * LANES
        k_pieces = [(cache_k, (1, 1, past, wd), lambda bi, h: (bi, layer, 0, h)),
                    (p3, (1, n, wd), lambda bi, h: (bi, 0, C_KA // wd + h))]
        v_pieces = [(cache_v, (1, 1, past, wd), lambda bi, h: (bi, layer, 0, h)),
                    (p3, (1, n, wd), lambda bi, h: (bi, 0, C_VA // wd + h))]
    oa = _diff_attention(p3, k_pieces, v_pieces, rope, attn_lambda, subln_g, layer, heads, min(ATT_SUB, n))
    orr, s_fin = _retention(p3, log_decay, layer, state_ret)
    op = _pool_mixer(p3, pool_w, pool_scale, layer)
    parts = [oa.reshape(b * n, WIDTH_A), orr.reshape(b * n, WIDTH_R), op.reshape(b * n, WIDTH_P)]
    x = _residual_projection(parts, w_out, layer, x, mod, mod_row, GATE1, TM_PROJ, TN_PROJ)
    a = _ffn_up(x, norm2_g, mod, mod_row, w_up, layer, conv_w, conv_b, n)
    x = _residual_projection([a], w_down, layer, x, mod, mod_row, GATE2, TM_DOWN, TN_DOWN)
    return x, p3, s_fin


def kernel(x_prompt, x_sample, cache_k, cache_v, state_ret, c, c_ctx, ada_w, ada_b, norm1_g, w_in,
           attn_lambda, attn_subln_g, ret_log_decay, pool_w, pool_scale, w_out, norm2_g,
           ffn_w_up, ffn_conv_w, ffn_conv_b, ffn_w_down, final_g):
    bp, sp, _ = x_prompt.shape
    bs, ss, _ = x_sample.shape
    past = cache_k.shape[2]
    ctx_row = bs
    assert bs < MOD_ROWS and x_prompt.shape[2] == D_MODEL and x_sample.shape[2] == D_MODEL
    for seq, tokens in ((sp, bp * sp), (ss, bs * ss)):
        assert TM_FFN % seq == 0 and tokens % TM_FFN == 0 and tokens % TM_PROJ == 0
        assert seq % RET_CHUNK == 0 and seq % min(ATT_SUB, seq) == 0 and seq % GRID_W == 0
    assert ss % TM_PROJ == 0 and ss % TM_DOWN == 0 and ss % TM_NORM == 0

    cs = jnp.concatenate([c, c_ctx[None, :], jnp.zeros((MOD_ROWS - bs - 1, D_MODEL), F32)], axis=0)
    mod = _modulation(cs, ada_w, ada_b).reshape(DEPTH, MOD_ROWS, 6, D_MODEL)

    rope = _rope_tables(ss)
    ck = cache_k.reshape(bs, DEPTH, past, QK_A)
    cv = cache_v.reshape(bs, DEPTH, past, WIDTH_A)

    w_in_b = w_in.astype(BF16)
    w_out_b = w_out.astype(BF16)
    w_up_b = ffn_w_up.astype(BF16)
    w_down_b = ffn_w_down.astype(BF16)
    norm1_3 = norm1_g.reshape(DEPTH, 1, D_MODEL)
    norm2_3 = norm2_g.reshape(DEPTH, 1, D_MODEL)
    wts = (norm1_3, w_in_b, attn_lambda, attn_subln_g.reshape(DEPTH, 1, DV_A), ret_log_decay, pool_w,
           pool_scale.reshape(DEPTH, 1, WIDTH_P), w_out_b, norm2_3, w_up_b, ffn_conv_w,
           ffn_conv_b.reshape(DEPTH, 1, D_FF), w_down_b)

    def prompt_row(i, tm):
        return ctx_row

    def sample_row(i, tm):
        return (i * tm) // ss

    xp = x_prompt.reshape(bp * sp, D_MODEL)
    xs = x_sample.reshape(bs * ss, D_MODEL)
    new_k, new_v, new_s = [], [], []
    for l in range(DEPTH):
        xp, p3, s_l = _trunk_layer(xp, l, bp, sp, mod, prompt_row, None, None, wts)
        new_k.append(p3[:, :, C_KA:C_KA + QK_A].reshape(bp, sp, 2 * N_HEADS_A, DH_A))
        new_v.append(p3[:, :, C_VA:C_VA + WIDTH_A].reshape(bp, sp, N_HEADS_A, DV_A))
        new_s.append(s_l)
        xs, _, _ = _trunk_layer(xs, l, bs, ss, mod, sample_row, rope, (ck, cv, state_ret), wts)
    y_prompt = _final_norm(xp, final_g).reshape(bp, sp, D_MODEL)
    y_sample = _final_norm(xs, final_g).reshape(bs, ss, D_MODEL)
    return (y_prompt, y_sample, jnp.stack(new_k, axis=1), jnp.stack(new_v, axis=1), jnp.stack(new_s, axis=1))
```

```python
import functools
import math

import jax
import jax.numpy as jnp
from jax import lax
from jax.experimental import pallas as pl
from jax.experimental.pallas import tpu as pltpu

F32 = jnp.float32
BF16 = jnp.bfloat16

D_MODEL = 2048
DEPTH = 2
GRID_W = 64
N_HEADS_A = 8
DH_A = 64
DV_A = 128
WIDTH_A = 1024
QK_A = 1024
N_HEADS_R = 4
DK_R = 64
DV_R = 128
WIDTH_R = 512
QK_R = 256
POOL_WINDOWS = (2, 4, 8, 16)
POOL_GROUP_DIM = 128
WIDTH_P = 512
IN_WIDTH = 5632
D_FF = 5632
ROPE_THETA = 10000.0
EPS = 1e-6

C_QA, C_KA, C_VA = 0, 1024, 2048
R_QR, R_KR, R_VR, R_GF, R_GB, R_POOL = 3072, 3328, 3584, 4096, 4608, 5120

SHIFT1, SCALE1, GATE1, SHIFT2, SCALE2, GATE2 = range(6)
MOD_ROWS = 16
LANES = 128
VMEM_LIMIT = 52 * 1024 * 1024
DOT_ROWS = 1024

TM_PROJ, TN_PROJ = 2048, 512
TN_WIDE = 1024
TM_DOWN, TN_DOWN = 1024, 512
TM_FFN, TN_FFN = 2048, 512
TM_NORM = 1024
NORM_ROWS = 16
RET_CHUNK = 256
ATT_SUB = 256
ATT_HEADS_LATENT = 2


def _cp(*sem):
    return pltpu.CompilerParams(dimension_semantics=sem, vmem_limit_bytes=VMEM_LIMIT)


def _silu(x):
    return x * (1.0 / (1.0 + jnp.exp(-x)))


def _dot_nt(a, b):
    return lax.dot_general(a, b, (((1,), (1,)), ((), ())), preferred_element_type=F32)


def _dot(a, b):
    return jnp.dot(a, b, preferred_element_type=F32)


def _row_chunks(tm):
    return [(r, min(DOT_ROWS, tm - r)) for r in range(0, tm, DOT_ROWS)]


def _mod_kernel(c_ref, w_ref, b_ref, o_ref):
    a = _silu(c_ref[...]).astype(BF16)
    o_ref[0] = _dot(a, w_ref[0].astype(BF16)) + b_ref[0]


def _modulation(cs, ada_w, ada_b):
    tn = 1024
    n = ada_w.shape[-1]
    return pl.pallas_call(
        _mod_kernel,
        grid=(DEPTH, n // tn),
        in_specs=[
            pl.BlockSpec((MOD_ROWS, D_MODEL), lambda l, j: (0, 0)),
            pl.BlockSpec((1, D_MODEL, tn), lambda l, j: (l, 0, j)),
            pl.BlockSpec((1, 1, tn), lambda l, j: (l, 0, j)),
        ],
        out_specs=pl.BlockSpec((1, MOD_ROWS, tn), lambda l, j: (l, 0, j)),
        out_shape=jax.ShapeDtypeStruct((DEPTH, MOD_ROWS, n), F32),
        compiler_params=_cp("arbitrary", "arbitrary"),
        name="modulation",
    )(cs, ada_w, ada_b.reshape(DEPTH, 1, n))


def _norm_mod_kernel(x_ref, g_ref, mod_ref, o_ref, gain_ref, *, shift_idx):
    gain_ref[0:1, :] = g_ref[0] * (1.0 + mod_ref[0, 0, shift_idx + 1:shift_idx + 2, :])
    gain_ref[1:2, :] = mod_ref[0, 0, shift_idx:shift_idx + 1, :]

    def body(c, carry):
        rows = pl.ds(pl.multiple_of(c * NORM_ROWS, NORM_ROWS), NORM_ROWS)
        x = x_ref[rows, :]
        r = lax.rsqrt(jnp.mean(x * x, axis=-1, keepdims=True) + EPS)
        o_ref[rows, :] = ((x * r) * gain_ref[0:1, :] + gain_ref[1:2, :]).astype(o_ref.dtype)
        return carry

    lax.fori_loop(0, x_ref.shape[0] // NORM_ROWS, body, 0, unroll=4)


def _norm_mod(x, norm_g, layer, mod, mod_row, shift_idx):
    t = x.shape[0]
    tm = TM_NORM
    return pl.pallas_call(
        functools.partial(_norm_mod_kernel, shift_idx=shift_idx),
        grid=(t // tm,),
        in_specs=[
            pl.BlockSpec((tm, D_MODEL), lambda i: (i, 0)),
            pl.BlockSpec((1, 1, D_MODEL), lambda i: (layer, 0, 0)),
            pl.BlockSpec((1, 1, 6, D_MODEL), lambda i: (layer, mod_row(i, tm), 0, 0)),
        ],
        out_specs=pl.BlockSpec((tm, D_MODEL), lambda i: (i, 0)),
        out_shape=jax.ShapeDtypeStruct((t, D_MODEL), BF16),
        scratch_shapes=[pltpu.VMEM((8, D_MODEL), F32)],
        compiler_params=_cp("arbitrary"),
        name="norm_mod",
    )(x, norm_g, mod)


def _final_norm_kernel(x_ref, g_ref, o_ref):
    x = x_ref[...]
    y = x * lax.rsqrt(jnp.mean(x * x, axis=-1, keepdims=True) + EPS)
    o_ref[...] = y * g_ref[...]


def _final_norm(x, g):
    t = x.shape[0]
    tm = TM_NORM
    return pl.pallas_call(
        _final_norm_kernel,
        grid=(t // tm,),
        in_specs=[
            pl.BlockSpec((tm, D_MODEL), lambda i: (i, 0)),
            pl.BlockSpec((1, D_MODEL), lambda i: (0, 0)),
        ],
        out_specs=pl.BlockSpec((tm, D_MODEL), lambda i: (i, 0)),
        out_shape=jax.ShapeDtypeStruct((t, D_MODEL), F32),
        compiler_params=_cp("arbitrary"),
        name="final_norm",
    )(x, g.reshape(1, D_MODEL))


def _rope_tile(x, cos, sin_signed, first_half):
    swapped = jnp.where(first_half, pltpu.roll(x, LANES - 16, 1), pltpu.roll(x, 16, 1))
    return x * cos + swapped * sin_signed


def _mm_kernel(a_ref, w_ref, o_ref):
    for c0 in range(0, o_ref.shape[1], TN_PROJ):
        w = w_ref[0, :, c0:c0 + TN_PROJ]
        for r0, m in _row_chunks(a_ref.shape[0]):
            o_ref[r0:r0 + m, c0:c0 + TN_PROJ] = _dot(a_ref[r0:r0 + m, :], w)


def _in_projection(h, w_in, layer, col0, ncols, tn):
    t = h.shape[0]
    tm = TM_PROJ
    j0 = col0 // tn
    return pl.pallas_call(
        _mm_kernel,
        grid=(t // tm, ncols // tn),
        in_specs=[
            pl.BlockSpec((tm, D_MODEL), lambda i, j: (i, 0)),
            pl.BlockSpec((1, D_MODEL, tn), lambda i, j: (layer, 0, j0 + j)),
        ],
        out_specs=pl.BlockSpec((tm, tn), lambda i, j: (i, j)),
        out_shape=jax.ShapeDtypeStruct((t, ncols), F32),
        compiler_params=_cp("arbitrary", "arbitrary"),
        name="in_projection",
    )(h, w_in)


def _rope_tables(n):
    rows = n // GRID_W
    row = jnp.repeat(jnp.arange(rows, dtype=F32), GRID_W)
    col = jnp.tile(jnp.arange(GRID_W, dtype=F32), rows)
    n_freq = DH_A // 4
    inv = ROPE_THETA ** (-jnp.arange(n_freq, dtype=F32) / n_freq)
    ang_r = row[:, None] * inv
    ang_c = col[:, None] * inv
    cos64 = jnp.concatenate([jnp.cos(ang_r), jnp.cos(ang_r), jnp.cos(ang_c), jnp.cos(ang_c)], -1)
    sin64 = jnp.concatenate([-jnp.sin(ang_r), jnp.sin(ang_r), -jnp.sin(ang_c), jnp.sin(ang_c)], -1)
    return jnp.tile(cos64, (1, 2)), jnp.tile(sin64, (1, 2))


def _mm_res_kernel(*refs, n_a, gate_idx):
    a_refs = refs[:n_a]
    w_ref, x_ref, mod_ref, o_ref = refs[n_a:]
    gate = mod_ref[0, 0, gate_idx:gate_idx + 1, :]
    for r0, m in _row_chunks(x_ref.shape[0]):
        rows = slice(r0, r0 + m)
        acc = None
        k0 = 0
        for a_ref in a_refs:
            kw = a_ref.shape[1]
            part = _dot(a_ref[rows, :], w_ref[0, k0:k0 + kw, :])
            acc = part if acc is None else acc + part
            k0 += kw
        o_ref[rows, :] = x_ref[rows, :] + gate * acc


def _residual_projection(a_parts, w, layer, x, mod, mod_row, gate_idx, tm, tn):
    t = x.shape[0]
    k, n = w.shape[1], w.shape[2]
    return pl.pallas_call(
        functools.partial(_mm_res_kernel, n_a=len(a_parts), gate_idx=gate_idx),
        grid=(t // tm, n // tn),
        in_specs=[pl.BlockSpec((tm, a.shape[1]), lambda i, j: (i, 0)) for a in a_parts] + [
            pl.BlockSpec((1, k, tn), lambda i, j: (layer, 0, j)),
            pl.BlockSpec((tm, tn), lambda i, j: (i, j)),
            pl.BlockSpec((1, 1, 6, tn), lambda i, j: (layer, mod_row(i, tm), 0, j)),
        ],
        out_specs=pl.BlockSpec((tm, tn), lambda i, j: (i, j)),
        out_shape=jax.ShapeDtypeStruct((t, n), F32),
        compiler_params=_cp("arbitrary", "arbitrary"),
        name="residual_projection",
    )(*a_parts, w, x, mod)


def _attn_kernel(*refs, lam_init, heads, sub, n_kv, rope):
    q_ref = refs[0]
    k_refs = refs[1:1 + n_kv]
    v_refs = refs[1 + n_kv:1 + 2 * n_kv]
    if rope:
        cos_ref, sin_ref, lam_ref, g_ref, o_ref = refs[1 + 2 * n_kv:]
    else:
        lam_ref, g_ref, o_ref = refs[1 + 2 * n_kv:]

    def rows_of(ref, cols):
        return ref[(0,) * (len(ref.shape) - 2) + (slice(None), cols)]

    def rotated(x, r0):
        m = x.shape[0]
        first_half = (lax.broadcasted_iota(jnp.int32, (m, LANES), 1) % 32) < 16
        return _rope_tile(x, cos_ref[r0:r0 + m, :], sin_ref[r0:r0 + m, :], first_half)

    lp = lam_ref[0]
    lam = (jnp.exp(jnp.sum(lp[0:1] * lp[1:2], axis=-1, keepdims=True))
           - jnp.exp(jnp.sum(lp[2:3] * lp[3:4], axis=-1, keepdims=True)) + lam_init)
    tq = q_ref.shape[1]
    nk = sum(r.shape[-2] for r in k_refs)
    gain = g_ref[0] * (1.0 - lam_init)
    edge = sub // 2 if tq > 2 * sub else sub
    bounds = [0] + list(range(edge, tq - edge + 1, sub)) + [tq]
    ones = jnp.ones((nk, LANES), BF16)
    for h in range(heads):
        cols = slice(h * LANES, (h + 1) * LANES)
        k_parts = [rows_of(r, cols) for r in k_refs]
        if rope:
            k_parts[-1] = rotated(k_parts[-1], 0)
        k = jnp.concatenate([kp.astype(BF16) for kp in k_parts], axis=0)
        v1 = jnp.concatenate([rows_of(r, cols).astype(BF16) for r in v_refs] , axis=0)
        v1 = jnp.concatenate([v1, ones], axis=1)
        for r0, r1 in zip(bounds[:-1], bounds[1:]):
            m = r1 - r0
            q = q_ref[0, r0:r1, cols]
            if rope:
                q = rotated(q, r0)
            q = (q * (DH_A ** -0.5)).astype(BF16)
            zero = jnp.zeros_like(q)
            lane = lax.broadcasted_iota(jnp.int32, (m, LANES), 1)
            qq = jnp.concatenate([jnp.where(lane < DH_A, q, zero), jnp.where(lane >= DH_A, q, zero)], axis=0)
            s = _dot_nt(qq, k)
            e = jnp.exp(s - jnp.max(s, axis=-1, keepdims=True)).astype(BF16)
            ov = _dot(e, v1)
            p = ov[:, :LANES] / ov[:, LANES:]
            o = p[:m] - lam * p[m:]
            o = o * lax.rsqrt(jnp.mean(o * o, axis=-1, keepdims=True) + EPS)
            o_ref[0, r0:r1, cols] = (o * gain).astype(o_ref.dtype)


def _diff_attention(p3, k_pieces, v_pieces, rope, attn_lambda, subln_g, layer, heads, sub):
    b, nq = p3.shape[0], p3.shape[1]
    lam_init = 0.8 - 0.6 * math.exp(-0.3 * layer)
    wd = heads * LANES
    specs = [pl.BlockSpec(shape, imap) for _, shape, imap in k_pieces + v_pieces]
    args = [arr for arr, _, _ in k_pieces + v_pieces]
    if rope is not None:
        specs += [pl.BlockSpec((nq, LANES), lambda bi, h: (0, 0))] * 2
        args += list(rope)
    return pl.pallas_call(
        functools.partial(_attn_kernel, lam_init=lam_init, heads=heads, sub=sub, n_kv=len(k_pieces),
                          rope=rope is not None),
        grid=(b, N_HEADS_A // heads),
        in_specs=[pl.BlockSpec((1, nq, wd), lambda bi, h: (bi, 0, h))] + specs + [
            pl.BlockSpec((1, 4, DH_A), lambda bi, h: (layer, 0, 0)),
            pl.BlockSpec((1, 1, DV_A), lambda bi, h: (layer, 0, 0)),
        ],
        out_specs=pl.BlockSpec((1, nq, wd), lambda bi, h: (bi, 0, h)),
        out_shape=jax.ShapeDtypeStruct((b, nq, WIDTH_A), BF16),
        compiler_params=_cp("arbitrary", "arbitrary"),
        name="diff_attention",
    )(p3, *args, attn_lambda, subln_g)


def _ret_kernel(ld_ref, q_ref, k_ref, v_ref, gf_ref, gb_ref, *rest, n, c, has_s0, layer):
    if has_s0:
        s0_ref, o_ref, sfin_ref, mask_ref, dec_ref, cdec_ref, state_ref, of_ref = rest
    else:
        o_ref, sfin_ref, mask_ref, dec_ref, cdec_ref, state_ref, of_ref = rest
    nc = n // c
    hq = QK_R

    lane_head = lax.broadcasted_iota(jnp.int32, (c, hq), 1) // DK_R
    row_head = lax.broadcasted_iota(jnp.int32, (hq, c), 0) // DK_R

    @pl.when(pl.program_id(0) == 0)
    def _():
        ri = lax.broadcasted_iota(jnp.int32, (c, c), 0)
        ci = lax.broadcasted_iota(jnp.int32, (c, c), 1)
        diff = (ri - ci).astype(F32)
        rowpos = lax.broadcasted_iota(jnp.int32, (c, hq), 0).astype(F32)
        for d in range(2):
            lg_lane = jnp.zeros((c, hq), F32)
            for h in range(N_HEADS_R):
                lg = ld_ref[layer, d, h]
                lg_lane = jnp.where(lane_head == h, lg, lg_lane)
                dist = diff if d == 0 else -diff
                mask_ref[d, h] = jnp.where(dist >= 0, jnp.exp(lg * jnp.maximum(dist, 0.0)), 0.0)
                cdec_ref[d, h] = jnp.exp(jnp.full((8, DV_R), lg * c, F32))
            if d == 0:
                dec_ref[0] = jnp.exp(lg_lane * (c - 1.0 - rowpos))
                dec_ref[1] = jnp.exp(lg_lane * (rowpos + 1.0))
            else:
                dec_ref[2] = jnp.exp(lg_lane * rowpos)
                dec_ref[3] = jnp.exp(lg_lane * (c - rowpos))

    rowh = lax.broadcasted_iota(jnp.int32, (hq, DV_R), 0) // DK_R
    for d in range(2):
        for h in range(N_HEADS_R):
            if has_s0:
                s0 = s0_ref[0, 0, d].reshape(hq, DV_R)
                state_ref[d, h] = jnp.where(rowh == h, s0, 0.0)
            else:
                state_ref[d, h] = jnp.zeros((hq, DV_R), F32)

    def chunk(ci_, d):
        start = pl.multiple_of(ci_ * c, c)
        rows = pl.ds(start, c)
        qc = q_ref[0, rows, :]
        kc = k_ref[0, rows, :] * (DK_R ** -0.5)
        kb = kc.astype(BF16)
        kdec_t = (kc * dec_ref[2 * d]).T.astype(BF16)
        qdec = (qc * dec_ref[2 * d + 1]).astype(BF16)
        qb = qc.astype(BF16)
        outs = []
        for h in range(N_HEADS_R):
            vh = v_ref[0, rows, h * DV_R:(h + 1) * DV_R].astype(BF16)
            s = _dot_nt(jnp.where(lane_head == h, qb, jnp.zeros_like(qb)), kb)
            o = _dot((s * mask_ref[d, h]).astype(BF16), vh)
            st = state_ref[d, h]
            o = o + _dot(jnp.where(lane_head == h, qdec, jnp.zeros_like(qdec)), st.astype(BF16))
            kv = _dot(jnp.where(row_head == h, kdec_t, jnp.zeros_like(kdec_t)), vh)
            state_ref[d, h] = st * cdec_ref[d, h, 0:1, :] + kv
            outs.append(o)
        return rows, outs

    def fwd_body(ci_, carry):
        rows, outs = chunk(ci_, 0)
        for h in range(N_HEADS_R):
            of_ref[rows, h * DV_R:(h + 1) * DV_R] = outs[h]
        return carry

    lax.fori_loop(0, nc, fwd_body, 0)

    def bwd_body(i, carry):
        rows, outs = chunk(nc - 1 - i, 1)
        for h in range(N_HEADS_R):
            sl = slice(h * DV_R, (h + 1) * DV_R)
            of = of_ref[rows, sl]
            ob = outs[h]
            yf = of * lax.rsqrt(jnp.mean(of * of, axis=-1, keepdims=True) + EPS)
            yb = ob * lax.rsqrt(jnp.mean(ob * ob, axis=-1, keepdims=True) + EPS)
            y = _silu(gf_ref[0, rows, sl]) * yf + _silu(gb_ref[0, rows, sl]) * yb
            o_ref[0, rows, sl] = y.astype(o_ref.dtype)
        return carry

    lax.fori_loop(0, nc, bwd_body, 0)

    for d in range(2):
        for h in range(N_HEADS_R):
            sfin_ref[0, d, h] = state_ref[d, h, h * DK_R:(h + 1) * DK_R, :]


def _retention(rest3, base, log_decay, layer, state_ret):
    b, n = rest3.shape[0], rest3.shape[1]
    c = RET_CHUNK
    has_s0 = state_ret is not None
    in_specs = [
        pl.BlockSpec(memory_space=pltpu.SMEM),
        pl.BlockSpec((1, n, QK_R), lambda bi: (bi, 0, (R_QR - base) // QK_R)),
        pl.BlockSpec((1, n, QK_R), lambda bi: (bi, 0, (R_KR - base) // QK_R)),
        pl.BlockSpec((1, n, WIDTH_R), lambda bi: (bi, 0, (R_VR - base) // WIDTH_R)),
        pl.BlockSpec((1, n, WIDTH_R), lambda bi: (bi, 0, (R_GF - base) // WIDTH_R)),
        pl.BlockSpec((1, n, WIDTH_R), lambda bi: (bi, 0, (R_GB - base) // WIDTH_R)),
    ]
    args = [log_decay, rest3, rest3, rest3, rest3, rest3]
    if has_s0:
        in_specs.append(pl.BlockSpec((1, 1, 2, N_HEADS_R, DK_R, DV_R), lambda bi: (bi, layer, 0, 0, 0, 0)))
        args.append(state_ret)
    return pl.pallas_call(
        functools.partial(_ret_kernel, n=n, c=c, has_s0=has_s0, layer=layer),
        grid=(b,),
        in_specs=in_specs,
        out_specs=[
            pl.BlockSpec((1, n, WIDTH_R), lambda bi: (bi, 0, 0)),
            pl.BlockSpec((1, 2, N_HEADS_R, DK_R, DV_R), lambda bi: (bi, 0, 0, 0, 0)),
        ],
        out_shape=[
            jax.ShapeDtypeStruct((b, n, WIDTH_R), BF16),
            jax.ShapeDtypeStruct((b, 2, N_HEADS_R, DK_R, DV_R), F32),
        ],
        scratch_shapes=[
            pltpu.VMEM((2, N_HEADS_R, c, c), F32),
            pltpu.VMEM((4, c, QK_R), F32),
            pltpu.VMEM((2, N_HEADS_R, 8, DV_R), F32),
            pltpu.VMEM((2, N_HEADS_R, QK_R, DV_R), F32),
            pltpu.VMEM((n, WIDTH_R), F32),
        ],
        compiler_params=_cp("arbitrary"),
        name="retention",
    )(*args)


def _pool_kernel(x_ref, w_ref, sc_ref, o_ref, *, n):
    t = lax.broadcasted_iota(jnp.int32, (n, POOL_GROUP_DIM), 0)

    def down(a, s):
        return jnp.where(t >= s, pltpu.roll(a, s, 0), 0.0)

    def up(a, s):
        return jnp.where(t < n - s, pltpu.roll(a, n - s, 0), 0.0)

    for g, win in enumerate(POOL_WINDOWS):
        sl = slice(g * POOL_GROUP_DIM, (g + 1) * POOL_GROUP_DIM)
        x = x_ref[0, :, sl]
        half = win // 2
        lead = x
        trail = down(x, 1)
        w = 1
        while w < half:
            lead = lead + up(lead, w)
            trail = trail + down(trail, w)
            w *= 2
        lo = jnp.clip(t - half, 0, n)
        hi = jnp.clip(t - half + win, 0, n)
        pooled = (lead + trail) / (hi - lo).astype(F32) - x
        y = _dot(pooled.astype(BF16), w_ref[0, g].astype(BF16))
        o_ref[0, :, sl] = (y * sc_ref[0, :, sl]).astype(o_ref.dtype)


def _pool_mixer(rest3, base, pool_w, pool_scale, layer):
    b, n = rest3.shape[0], rest3.shape[1]
    ng = len(POOL_WINDOWS)
    return pl.pallas_call(
        functools.partial(_pool_kernel, n=n),
        grid=(b,),
        in_specs=[
            pl.BlockSpec((1, n, WIDTH_P), lambda bi: (bi, 0, (R_POOL - base) // WIDTH_P)),
            pl.BlockSpec((1, ng, POOL_GROUP_DIM, POOL_GROUP_DIM), lambda bi: (layer, 0, 0, 0)),
            pl.BlockSpec((1, 1, WIDTH_P), lambda bi: (layer, 0, 0)),
        ],
        out_specs=pl.BlockSpec((1, n, WIDTH_P), lambda bi: (bi, 0, 0)),
        out_shape=jax.ShapeDtypeStruct((b, n, WIDTH_P), BF16),
        compiler_params=_cp("arbitrary"),
        name="pool_mixer",
    )(rest3, pool_w, pool_scale)


def _ffn_up_kernel(h_ref, wg_ref, wv_ref, cw_ref, cb_ref, o_ref, *, seq):
    wg = wg_ref[0]
    wv = wv_ref[0]
    chunks = _row_chunks(h_ref.shape[0])
    g = jnp.concatenate([_dot(h_ref[r:r + m, :], wg) for r, m in chunks], axis=0)
    v = jnp.concatenate([_dot(h_ref[r:r + m, :], wv) for r, m in chunks], axis=0)
    tm = g.shape[0]
    pos = lax.broadcasted_iota(jnp.int32, g.shape, 0) % seq
    g_prev = jnp.where(pos == 0, 0.0, pltpu.roll(g, 1, 0))
    g_next = jnp.where(pos == seq - 1, 0.0, pltpu.roll(g, tm - 1, 0))
    gc = g_prev * cw_ref[0, 0:1, :] + g * cw_ref[0, 1:2, :] + g_next * cw_ref[0, 2:3, :] + cb_ref[0]
    o_ref[...] = (_silu(gc) * v).astype(o_ref.dtype)


def _ffn_up(h, w_up, layer, conv_w, conv_b, seq):
    t = h.shape[0]
    tm, tn = TM_FFN, TN_FFN
    nj = D_FF // tn
    return pl.pallas_call(
        functools.partial(_ffn_up_kernel, seq=seq),
        grid=(t // tm, nj),
        in_specs=[
            pl.BlockSpec((tm, D_MODEL), lambda i, j: (i, 0)),
            pl.BlockSpec((1, D_MODEL, tn), lambda i, j: (layer, 0, j)),
            pl.BlockSpec((1, D_MODEL, tn), lambda i, j: (layer, 0, j + nj)),
            pl.BlockSpec((1, 3, tn), lambda i, j: (layer, 0, j)),
            pl.BlockSpec((1, 1, tn), lambda i, j: (layer, 0, j)),
        ],
        out_specs=pl.BlockSpec((tm, tn), lambda i, j: (i, j)),
        out_shape=jax.ShapeDtypeStruct((t, D_FF), BF16),
        compiler_params=_cp("arbitrary", "arbitrary"),
        name="ffn_up_convglu",
    )(h, w_up, w_up, conv_w, conv_b)


def _trunk_layer(x, layer, b, n, mod, mod_row, rope, ctx, wts):
    (norm1_g, w_in, attn_lambda, subln_g, log_decay, pool_w, pool_scale, w_out, norm2_g, w_up, conv_w,
     conv_b, w_down) = wts
    h = _norm_mod(x, norm1_g, layer, mod, mod_row, SHIFT1)
    if rope is None:
        q3, k3, v3 = (_in_projection(h, w_in, layer, c0, WIDTH_A, TN_WIDE).reshape(b, n, WIDTH_A)
                      for c0 in (C_QA, C_KA, C_VA))
        rest_base = pool_base = R_QR
        rest3 = pool3 = _in_projection(h, w_in, layer, rest_base, IN_WIDTH - rest_base,
                                       TN_PROJ).reshape(b, n, -1)
        heads, state_ret = N_HEADS_A, None
        k_pieces = [(k3, (1, n, WIDTH_A), lambda bi, h: (bi, 0, 0))]
        v_pieces = [(v3, (1, n, WIDTH_A), lambda bi, h: (bi, 0, 0))]
    else:
        cache_k, cache_v, state_ret = ctx
        past = cache_k.shape[2]
        heads = ATT_HEADS_LATENT
        wd = heads * LANES
        q3 = rest3 = _in_projection(h, w_in, layer, 0, R_POOL, TN_WIDE).reshape(b, n, R_POOL)
        pool3 = _in_projection(h, w_in, layer, R_POOL, WIDTH_P, TN_PROJ).reshape(b, n, WIDTH_P)
        rest_base, pool_base = 0, R_POOL
        k3 = v3 = None
        k_pieces = [(cache_k, (1, 1, past, wd), lambda bi, h: (bi, layer, 0, h)),
                    (q3, (1, n, wd), lambda bi, h: (bi, 0, C_KA // wd + h))]
        v_pieces = [(cache_v, (1, 1, past, wd), lambda bi, h: (bi, layer, 0, h)),
                    (q3, (1, n, wd), lambda bi, h: (bi, 0, C_VA // wd + h))]
    oa = _diff_attention(q3, k_pieces, v_pieces, rope, attn_lambda, subln_g, layer, heads, min(ATT_SUB, n))
    orr, s_fin = _retention(rest3, rest_base, log_decay, layer, state_ret)
    op = _pool_mixer(pool3, pool_base, pool_w, pool_scale, layer)
    parts = [oa.reshape(b * n, WIDTH_A), orr.reshape(b * n, WIDTH_R), op.reshape(b * n, WIDTH_P)]
    x = _residual_projection(parts, w_out, layer, x, mod, mod_row, GATE1, TM_PROJ, TN_PROJ)
    h2 = _norm_mod(x, norm2_g, layer, mod, mod_row, SHIFT2)
    a = _ffn_up(h2, w_up, layer, conv_w, conv_b, n)
    x = _residual_projection([a], w_down, layer, x, mod, mod_row, GATE2, TM_DOWN, TN_DOWN)
    return x, k3, v3, s_fin


def kernel(x_prompt, x_sample, cache_k, cache_v, state_ret, c, c_ctx, ada_w, ada_b, norm1_g, w_in,
           attn_lambda, attn_subln_g, ret_log_decay, pool_w, pool_scale, w_out, norm2_g,
           ffn_w_up, ffn_conv_w, ffn_conv_b, ffn_w_down, final_g):
    bp, sp, _ = x_prompt.shape
    bs, ss, _ = x_sample.shape
    past = cache_k.shape[2]
    ctx_row = bs
    assert bs < MOD_ROWS and x_prompt.shape[2] == D_MODEL and x_sample.shape[2] == D_MODEL
    for seq, tokens in ((sp, bp * sp), (ss, bs * ss)):
        assert TM_FFN % seq == 0 and tokens % TM_FFN == 0 and tokens % TM_PROJ == 0
        assert seq % RET_CHUNK == 0 and seq % min(ATT_SUB, seq) == 0 and seq % GRID_W == 0
    assert ss % TM_PROJ == 0 and ss % TM_DOWN == 0 and ss % TM_NORM == 0

    cs = jnp.concatenate([c, c_ctx[None, :], jnp.zeros((MOD_ROWS - bs - 1, D_MODEL), F32)], axis=0)
    mod = _modulation(cs, ada_w, ada_b).reshape(DEPTH, MOD_ROWS, 6, D_MODEL)

    rope = _rope_tables(ss)
    ck = cache_k.reshape(bs, DEPTH, past, QK_A)
    cv = cache_v.reshape(bs, DEPTH, past, WIDTH_A)

    w_in_b = w_in.astype(BF16)
    w_out_b = w_out.astype(BF16)
    w_up_b = ffn_w_up.astype(BF16)
    w_down_b = ffn_w_down.astype(BF16)
    norm1_3 = norm1_g.reshape(DEPTH, 1, D_MODEL)
    norm2_3 = norm2_g.reshape(DEPTH, 1, D_MODEL)
    wts = (norm1_3, w_in_b, attn_lambda, attn_subln_g.reshape(DEPTH, 1, DV_A), ret_log_decay, pool_w,
           pool_scale.reshape(DEPTH, 1, WIDTH_P), w_out_b, norm2_3, w_up_b, ffn_conv_w,
           ffn_conv_b.reshape(DEPTH, 1, D_FF), w_down_b)

    def prompt_row(i, tm):
        return ctx_row

    def sample_row(i, tm):
        return (i * tm) // ss

    xp = x_prompt.reshape(bp * sp, D_MODEL)
    xs = x_sample.reshape(bs * ss, D_MODEL)
    new_k, new_v, new_s = [], [], []
    for l in range(DEPTH):
        xp, k_l, v_l, s_l = _trunk_layer(xp, l, bp, sp, mod, prompt_row, None, None, wts)
        new_k.append(k_l.reshape(bp, sp, 2 * N_HEADS_A, DH_A))
        new_v.append(v_l.reshape(bp, sp, N_HEADS_A, DV_A))
        new_s.append(s_l)
        xs, _, _, _ = _trunk_layer(xs, l, bs, ss, mod, sample_row, rope, (ck, cv, state_ret), wts)
    y_prompt = _final_norm(xp, final_g).reshape(bp, sp, D_MODEL)
    y_sample = _final_norm(xs, final_g).reshape(bs, ss, D_MODEL)
    return (y_prompt, y_sample, jnp.stack(new_k, axis=1), jnp.stack(new_v, axis=1), jnp.stack(new_s, axis=1))
```

```python
import functools
import math

import jax
import jax.numpy as jnp
from jax import lax
from jax.experimental import pallas as pl
from jax.experimental.pallas import tpu as pltpu

F32 = jnp.float32
BF16 = jnp.bfloat16

D_MODEL = 2048
DEPTH = 2
GRID_W = 64
N_HEADS_A = 8
DH_A = 64
DV_A = 128
WIDTH_A = 1024
QK_A = 1024
N_HEADS_R = 4
DK_R = 64
DV_R = 128
WIDTH_R = 512
QK_R = 256
POOL_WINDOWS = (2, 4, 8, 16)
POOL_GROUP_DIM = 128
WIDTH_P = 512
IN_WIDTH = 5632
D_FF = 5632
ROPE_THETA = 10000.0
EPS = 1e-6

C_QA, C_KA, C_VA = 0, 1024, 2048
R_QR, R_KR, R_VR, R_GF, R_GB, R_POOL = 3072, 3328, 3584, 4096, 4608, 5120

SHIFT1, SCALE1, GATE1, SHIFT2, SCALE2, GATE2 = range(6)
MOD_ROWS = 16
LANES = 128
VMEM_LIMIT = 52 * 1024 * 1024
DOT_ROWS = 1024

TM_PROJ, TN_PROJ = 2048, 512
TN_WIDE = 1024
TM_DOWN, TN_DOWN = 1024, 512
TM_FFN, TN_FFN = 2048, 512
TM_NORM = 1024
NORM_ROWS = 16
RET_CHUNK = 256
ATT_SUB = 256
ATT_HEADS_LATENT = 2


def _cp(*sem):
    return pltpu.CompilerParams(dimension_semantics=sem, vmem_limit_bytes=VMEM_LIMIT)


def _silu(x):
    return x * (1.0 / (1.0 + jnp.exp(-x)))


def _dot_nt(a, b):
    return lax.dot_general(a, b, (((1,), (1,)), ((), ())), preferred_element_type=F32)


def _dot(a, b):
    return jnp.dot(a, b, preferred_element_type=F32)


def _row_chunks(tm):
    return [(r, min(DOT_ROWS, tm - r)) for r in range(0, tm, DOT_ROWS)]


def _mod_kernel(c_ref, w_ref, b_ref, o_ref):
    a = _silu(c_ref[...]).astype(BF16)
    o_ref[0] = _dot(a, w_ref[0].astype(BF16)) + b_ref[0]


def _modulation(cs, ada_w, ada_b):
    tn = 1024
    n = ada_w.shape[-1]
    return pl.pallas_call(
        _mod_kernel,
        grid=(DEPTH, n // tn),
        in_specs=[
            pl.BlockSpec((MOD_ROWS, D_MODEL), lambda l, j: (0, 0)),
            pl.BlockSpec((1, D_MODEL, tn), lambda l, j: (l, 0, j)),
            pl.BlockSpec((1, 1, tn), lambda l, j: (l, 0, j)),
        ],
        out_specs=pl.BlockSpec((1, MOD_ROWS, tn), lambda l, j: (l, 0, j)),
        out_shape=jax.ShapeDtypeStruct((DEPTH, MOD_ROWS, n), F32),
        compiler_params=_cp("arbitrary", "arbitrary"),
        name="modulation",
    )(cs, ada_w, ada_b.reshape(DEPTH, 1, n))


def _norm_mod_kernel(x_ref, g_ref, mod_ref, o_ref, gain_ref, *, shift_idx):
    gain_ref[0:1, :] = g_ref[0] * (1.0 + mod_ref[0, 0, shift_idx + 1:shift_idx + 2, :])
    gain_ref[1:2, :] = mod_ref[0, 0, shift_idx:shift_idx + 1, :]

    def body(c, carry):
        rows = pl.ds(pl.multiple_of(c * NORM_ROWS, NORM_ROWS), NORM_ROWS)
        x = x_ref[rows, :]
        r = lax.rsqrt(jnp.mean(x * x, axis=-1, keepdims=True) + EPS)
        o_ref[rows, :] = ((x * r) * gain_ref[0:1, :] + gain_ref[1:2, :]).astype(o_ref.dtype)
        return carry

    lax.fori_loop(0, x_ref.shape[0] // NORM_ROWS, body, 0, unroll=4)


def _norm_mod(x, norm_g, layer, mod, mod_row, shift_idx):
    t = x.shape[0]
    tm = TM_NORM
    return pl.pallas_call(
        functools.partial(_norm_mod_kernel, shift_idx=shift_idx),
        grid=(t // tm,),
        in_specs=[
            pl.BlockSpec((tm, D_MODEL), lambda i: (i, 0)),
            pl.BlockSpec((1, 1, D_MODEL), lambda i: (layer, 0, 0)),
            pl.BlockSpec((1, 1, 6, D_MODEL), lambda i: (layer, mod_row(i, tm), 0, 0)),
        ],
        out_specs=pl.BlockSpec((tm, D_MODEL), lambda i: (i, 0)),
        out_shape=jax.ShapeDtypeStruct((t, D_MODEL), BF16),
        scratch_shapes=[pltpu.VMEM((8, D_MODEL), F32)],
        compiler_params=_cp("arbitrary"),
        name="norm_mod",
    )(x, norm_g, mod)


def _final_norm_kernel(x_ref, g_ref, o_ref):
    x = x_ref[...]
    y = x * lax.rsqrt(jnp.mean(x * x, axis=-1, keepdims=True) + EPS)
    o_ref[...] = y * g_ref[...]


def _final_norm(x, g):
    t = x.shape[0]
    tm = TM_NORM
    return pl.pallas_call(
        _final_norm_kernel,
        grid=(t // tm,),
        in_specs=[
            pl.BlockSpec((tm, D_MODEL), lambda i: (i, 0)),
            pl.BlockSpec((1, D_MODEL), lambda i: (0, 0)),
        ],
        out_specs=pl.BlockSpec((tm, D_MODEL), lambda i: (i, 0)),
        out_shape=jax.ShapeDtypeStruct((t, D_MODEL), F32),
        compiler_params=_cp("arbitrary"),
        name="final_norm",
    )(x, g.reshape(1, D_MODEL))


def _rope_tile(x, cos, sin_signed, first_half):
    swapped = jnp.where(first_half, pltpu.roll(x, LANES - 16, 1), pltpu.roll(x, 16, 1))
    return x * cos + swapped * sin_signed


def _mm_kernel(a_ref, w_ref, o_ref):
    for c0 in range(0, o_ref.shape[1], TN_PROJ):
        w = w_ref[0, :, c0:c0 + TN_PROJ]
        for r0, m in _row_chunks(a_ref.shape[0]):
            o_ref[r0:r0 + m, c0:c0 + TN_PROJ] = _dot(a_ref[r0:r0 + m, :], w)


def _in_projection(h, w_in, layer, col0, ncols, tn):
    t = h.shape[0]
    tm = TM_PROJ
    j0 = col0 // tn
    return pl.pallas_call(
        _mm_kernel,
        grid=(t // tm, ncols // tn),
        in_specs=[
            pl.BlockSpec((tm, D_MODEL), lambda i, j: (i, 0)),
            pl.BlockSpec((1, D_MODEL, tn), lambda i, j: (layer, 0, j0 + j)),
        ],
        out_specs=pl.BlockSpec((tm, tn), lambda i, j: (i, j)),
        out_shape=jax.ShapeDtypeStruct((t, ncols), F32),
        compiler_params=_cp("arbitrary", "arbitrary"),
        name="in_projection",
    )(h, w_in)


def _rope_tables(n):
    rows = n // GRID_W
    row = jnp.repeat(jnp.arange(rows, dtype=F32), GRID_W)
    col = jnp.tile(jnp.arange(GRID_W, dtype=F32), rows)
    n_freq = DH_A // 4
    inv = ROPE_THETA ** (-jnp.arange(n_freq, dtype=F32) / n_freq)
    ang_r = row[:, None] * inv
    ang_c = col[:, None] * inv
    cos64 = jnp.concatenate([jnp.cos(ang_r), jnp.cos(ang_r), jnp.cos(ang_c), jnp.cos(ang_c)], -1)
    sin64 = jnp.concatenate([-jnp.sin(ang_r), jnp.sin(ang_r), -jnp.sin(ang_c), jnp.sin(ang_c)], -1)
    return jnp.tile(cos64, (1, 2)), jnp.tile(sin64, (1, 2))


def _mm_res_kernel(*refs, n_a, gate_idx):
    a_refs = refs[:n_a]
    w_ref, x_ref, mod_ref, o_ref = refs[n_a:]
    gate = mod_ref[0, 0, gate_idx:gate_idx + 1, :]
    for r0, m in _row_chunks(x_ref.shape[0]):
        rows = slice(r0, r0 + m)
        acc = None
        k0 = 0
        for a_ref in a_refs:
            kw = a_ref.shape[1]
            part = _dot(a_ref[rows, :], w_ref[0, k0:k0 + kw, :])
            acc = part if acc is None else acc + part
            k0 += kw
        o_ref[rows, :] = x_ref[rows, :] + gate * acc


def _residual_projection(a_parts, w, layer, x, mod, mod_row, gate_idx, tm, tn):
    t = x.shape[0]
    k, n = w.shape[1], w.shape[2]
    return pl.pallas_call(
        functools.partial(_mm_res_kernel, n_a=len(a_parts), gate_idx=gate_idx),
        grid=(t // tm, n // tn),
        in_specs=[pl.BlockSpec((tm, a.shape[1]), lambda i, j: (i, 0)) for a in a_parts] + [
            pl.BlockSpec((1, k, tn), lambda i, j: (layer, 0, j)),
            pl.BlockSpec((tm, tn), lambda i, j: (i, j)),
            pl.BlockSpec((1, 1, 6, tn), lambda i, j: (layer, mod_row(i, tm), 0, j)),
        ],
        out_specs=pl.BlockSpec((tm, tn), lambda i, j: (i, j)),
        out_shape=jax.ShapeDtypeStruct((t, n), F32),
        compiler_params=_cp("arbitrary", "arbitrary"),
        name="residual_projection",
    )(*a_parts, w, x, mod)


def _residual_projection_streamed(a_parts, w, layer, x, mod, mod_row, gate_idx, tm, tn):
    t = x.shape[0]
    k, n = w.shape[1], w.shape[2]
    deep = pl.Buffered(3)
    in_specs = [pl.BlockSpec((tm, a.shape[1]), lambda i, j: (i, 0)) for a in a_parts] + [
        pl.BlockSpec((1, k, tn), lambda i, j: (layer, 0, j), pipeline_mode=deep),
        pl.BlockSpec((tm, tn), lambda i, j: (i, j), pipeline_mode=deep),
        pl.BlockSpec((1, 1, 6, tn), lambda i, j: (layer, mod_row(i, tm), 0, j)),
    ]
    body = functools.partial(_mm_res_kernel, n_a=len(a_parts), gate_idx=gate_idx)

    def outer(*refs):
        pltpu.emit_pipeline(body, grid=(t // tm, n // tn), in_specs=in_specs,
                            out_specs=[pl.BlockSpec((tm, tn), lambda i, j: (i, j))])(*refs)

    n_in = len(a_parts) + 3
    return pl.pallas_call(
        outer,
        in_specs=[pl.BlockSpec(memory_space=pl.ANY)] * n_in,
        out_specs=pl.BlockSpec(memory_space=pl.ANY),
        out_shape=jax.ShapeDtypeStruct((t, n), F32),
        compiler_params=pltpu.CompilerParams(vmem_limit_bytes=VMEM_LIMIT),
        name="residual_projection_streamed",
    )(*a_parts, w, x, mod)


def _attn_kernel(*refs, lam_init, heads, sub, n_kv, rope):
    q_ref = refs[0]
    k_refs = refs[1:1 + n_kv]
    v_refs = refs[1 + n_kv:1 + 2 * n_kv]
    if rope:
        cos_ref, sin_ref, lam_ref, g_ref, o_ref = refs[1 + 2 * n_kv:]
    else:
        lam_ref, g_ref, o_ref = refs[1 + 2 * n_kv:]

    def rows_of(ref, cols):
        return ref[(0,) * (len(ref.shape) - 2) + (slice(None), cols)]

    def rotated(x, r0):
        m = x.shape[0]
        first_half = (lax.broadcasted_iota(jnp.int32, (m, LANES), 1) % 32) < 16
        return _rope_tile(x, cos_ref[r0:r0 + m, :], sin_ref[r0:r0 + m, :], first_half)

    lp = lam_ref[0]
    lam = (jnp.exp(jnp.sum(lp[0:1] * lp[1:2], axis=-1, keepdims=True))
           - jnp.exp(jnp.sum(lp[2:3] * lp[3:4], axis=-1, keepdims=True)) + lam_init)
    tq = q_ref.shape[1]
    nk = sum(r.shape[-2] for r in k_refs)
    gain = g_ref[0] * (1.0 - lam_init)
    edge = sub // 2 if tq > 2 * sub else sub
    bounds = [0] + list(range(edge, tq - edge + 1, sub)) + [tq]
    ones = jnp.ones((nk, LANES), BF16)
    for h in range(heads):
        cols = slice(h * LANES, (h + 1) * LANES)
        k_parts = [rows_of(r, cols) for r in k_refs]
        if rope:
            k_parts[-1] = rotated(k_parts[-1], 0)
        k = jnp.concatenate([kp.astype(BF16) for kp in k_parts], axis=0)
        v1 = jnp.concatenate([rows_of(r, cols).astype(BF16) for r in v_refs] , axis=0)
        v1 = jnp.concatenate([v1, ones], axis=1)
        for r0, r1 in zip(bounds[:-1], bounds[1:]):
            m = r1 - r0
            q = q_ref[0, r0:r1, cols]
            if rope:
                q = rotated(q, r0)
            q = (q * (DH_A ** -0.5)).astype(BF16)
            zero = jnp.zeros_like(q)
            lane = lax.broadcasted_iota(jnp.int32, (m, LANES), 1)
            qq = jnp.concatenate([jnp.where(lane < DH_A, q, zero), jnp.where(lane >= DH_A, q, zero)], axis=0)
            s = _dot_nt(qq, k)
            e = jnp.exp(s - jnp.max(s, axis=-1, keepdims=True)).astype(BF16)
            ov = _dot(e, v1)
            p = ov[:, :LANES] / ov[:, LANES:]
            o = p[:m] - lam * p[m:]
            o = o * lax.rsqrt(jnp.mean(o * o, axis=-1, keepdims=True) + EPS)
            o_ref[0, r0:r1, cols] = (o * gain).astype(o_ref.dtype)


def _diff_attention(p3, k_pieces, v_pieces, rope, attn_lambda, subln_g, layer, heads, sub):
    b, nq = p3.shape[0], p3.shape[1]
    lam_init = 0.8 - 0.6 * math.exp(-0.3 * layer)
    wd = heads * LANES
    specs = [pl.BlockSpec(shape, imap) for _, shape, imap in k_pieces + v_pieces]
    args = [arr for arr, _, _ in k_pieces + v_pieces]
    if rope is not None:
        specs += [pl.BlockSpec((nq, LANES), lambda bi, h: (0, 0))] * 2
        args += list(rope)
    return pl.pallas_call(
        functools.partial(_attn_kernel, lam_init=lam_init, heads=heads, sub=sub, n_kv=len(k_pieces),
                          rope=rope is not None),
        grid=(b, N_HEADS_A // heads),
        in_specs=[pl.BlockSpec((1, nq, wd), lambda bi, h: (bi, 0, h))] + specs + [
            pl.BlockSpec((1, 4, DH_A), lambda bi, h: (layer, 0, 0)),
            pl.BlockSpec((1, 1, DV_A), lambda bi, h: (layer, 0, 0)),
        ],
        out_specs=pl.BlockSpec((1, nq, wd), lambda bi, h: (bi, 0, h)),
        out_shape=jax.ShapeDtypeStruct((b, nq, WIDTH_A), BF16),
        compiler_params=_cp("arbitrary", "arbitrary"),
        name="diff_attention",
    )(p3, *args, attn_lambda, subln_g)


def _ret_kernel(ld_ref, q_ref, k_ref, v_ref, gf_ref, gb_ref, *rest, n, c, has_s0, layer):
    if has_s0:
        s0_ref, o_ref, sfin_ref, mask_ref, dec_ref, cdec_ref, state_ref, of_ref = rest
    else:
        o_ref, sfin_ref, mask_ref, dec_ref, cdec_ref, state_ref, of_ref = rest
    nc = n // c
    hq = QK_R

    lane_head = lax.broadcasted_iota(jnp.int32, (c, hq), 1) // DK_R
    row_head = lax.broadcasted_iota(jnp.int32, (hq, c), 0) // DK_R

    @pl.when(pl.program_id(0) == 0)
    def _():
        ri = lax.broadcasted_iota(jnp.int32, (c, c), 0)
        ci = lax.broadcasted_iota(jnp.int32, (c, c), 1)
        diff = (ri - ci).astype(F32)
        rowpos = lax.broadcasted_iota(jnp.int32, (c, hq), 0).astype(F32)
        for d in range(2):
            lg_lane = jnp.zeros((c, hq), F32)
            for h in range(N_HEADS_R):
                lg = ld_ref[layer, d, h]
                lg_lane = jnp.where(lane_head == h, lg, lg_lane)
                dist = diff if d == 0 else -diff
                mask_ref[d, h] = jnp.where(dist >= 0, jnp.exp(lg * jnp.maximum(dist, 0.0)), 0.0)
                cdec_ref[d, h] = jnp.exp(jnp.full((8, DV_R), lg * c, F32))
            if d == 0:
                dec_ref[0] = jnp.exp(lg_lane * (c - 1.0 - rowpos))
                dec_ref[1] = jnp.exp(lg_lane * (rowpos + 1.0))
            else:
                dec_ref[2] = jnp.exp(lg_lane * rowpos)
                dec_ref[3] = jnp.exp(lg_lane * (c - rowpos))

    rowh = lax.broadcasted_iota(jnp.int32, (hq, DV_R), 0) // DK_R
    for d in range(2):
        for h in range(N_HEADS_R):
            if has_s0:
                s0 = s0_ref[0, 0, d].reshape(hq, DV_R)
                state_ref[d, h] = jnp.where(rowh == h, s0, 0.0)
            else:
                state_ref[d, h] = jnp.zeros((hq, DV_R), F32)

    def chunk(ci_, d):
        start = pl.multiple_of(ci_ * c, c)
        rows = pl.ds(start, c)
        qc = q_ref[0, rows, :]
        kc = k_ref[0, rows, :] * (DK_R ** -0.5)
        kb = kc.astype(BF16)
        kdec_t = (kc * dec_ref[2 * d]).T.astype(BF16)
        qdec = (qc * dec_ref[2 * d + 1]).astype(BF16)
        qb = qc.astype(BF16)
        outs = []
        for h in range(N_HEADS_R):
            vh = v_ref[0, rows, h * DV_R:(h + 1) * DV_R].astype(BF16)
            s = _dot_nt(jnp.where(lane_head == h, qb, jnp.zeros_like(qb)), kb)
            o = _dot((s * mask_ref[d, h]).astype(BF16), vh)
            st = state_ref[d, h]
            o = o + _dot(jnp.where(lane_head == h, qdec, jnp.zeros_like(qdec)), st.astype(BF16))
            kv = _dot(jnp.where(row_head == h, kdec_t, jnp.zeros_like(kdec_t)), vh)
            state_ref[d, h] = st * cdec_ref[d, h, 0:1, :] + kv
            outs.append(o)
        return rows, outs

    def fwd_body(ci_, carry):
        rows, outs = chunk(ci_, 0)
        for h in range(N_HEADS_R):
            of_ref[rows, h * DV_R:(h + 1) * DV_R] = outs[h]
        return carry

    lax.fori_loop(0, nc, fwd_body, 0)

    def bwd_body(i, carry):
        rows, outs = chunk(nc - 1 - i, 1)
        for h in range(N_HEADS_R):
            sl = slice(h * DV_R, (h + 1) * DV_R)
            of = of_ref[rows, sl]
            ob = outs[h]
            yf = of * lax.rsqrt(jnp.mean(of * of, axis=-1, keepdims=True) + EPS)
            yb = ob * lax.rsqrt(jnp.mean(ob * ob, axis=-1, keepdims=True) + EPS)
            y = _silu(gf_ref[0, rows, sl]) * yf + _silu(gb_ref[0, rows, sl]) * yb
            o_ref[0, rows, sl] = y.astype(o_ref.dtype)
        return carry

    lax.fori_loop(0, nc, bwd_body, 0)

    for d in range(2):
        for h in range(N_HEADS_R):
            sfin_ref[0, d, h] = state_ref[d, h, h * DK_R:(h + 1) * DK_R, :]


def _retention(rest3, base, log_decay, layer, state_ret):
    b, n = rest3.shape[0], rest3.shape[1]
    c = RET_CHUNK
    has_s0 = state_ret is not None
    in_specs = [
        pl.BlockSpec(memory_space=pltpu.SMEM),
        pl.BlockSpec((1, n, QK_R), lambda bi: (bi, 0, (R_QR - base) // QK_R)),
        pl.BlockSpec((1, n, QK_R), lambda bi: (bi, 0, (R_KR - base) // QK_R)),
        pl.BlockSpec((1, n, WIDTH_R), lambda bi: (bi, 0, (R_VR - base) // WIDTH_R)),
        pl.BlockSpec((1, n, WIDTH_R), lambda bi: (bi, 0, (R_GF - base) // WIDTH_R)),
        pl.BlockSpec((1, n, WIDTH_R), lambda bi: (bi, 0, (R_GB - base) // WIDTH_R)),
    ]
    args = [log_decay, rest3, rest3, rest3, rest3, rest3]
    if has_s0:
        in_specs.append(pl.BlockSpec((1, 1, 2, N_HEADS_R, DK_R, DV_R), lambda bi: (bi, layer, 0, 0, 0, 0)))
        args.append(state_ret)
    return pl.pallas_call(
        functools.partial(_ret_kernel, n=n, c=c, has_s0=has_s0, layer=layer),
        grid=(b,),
        in_specs=in_specs,
        out_specs=[
            pl.BlockSpec((1, n, WIDTH_R), lambda bi: (bi, 0, 0)),
            pl.BlockSpec((1, 2, N_HEADS_R, DK_R, DV_R), lambda bi: (bi, 0, 0, 0, 0)),
        ],
        out_shape=[
            jax.ShapeDtypeStruct((b, n, WIDTH_R), BF16),
            jax.ShapeDtypeStruct((b, 2, N_HEADS_R, DK_R, DV_R), F32),
        ],
        scratch_shapes=[
            pltpu.VMEM((2, N_HEADS_R, c, c), F32),
            pltpu.VMEM((4, c, QK_R), F32),
            pltpu.VMEM((2, N_HEADS_R, 8, DV_R), F32),
            pltpu.VMEM((2, N_HEADS_R, QK_R, DV_R), F32),
            pltpu.VMEM((n, WIDTH_R), F32),
        ],
        compiler_params=_cp("arbitrary"),
        name="retention",
    )(*args)


def _pool_kernel(x_ref, w_ref, sc_ref, o_ref, *, n):
    t = lax.broadcasted_iota(jnp.int32, (n, POOL_GROUP_DIM), 0)

    def down(a, s):
        return jnp.where(t >= s, pltpu.roll(a, s, 0), 0.0)

    def up(a, s):
        return jnp.where(t < n - s, pltpu.roll(a, n - s, 0), 0.0)

    for g, win in enumerate(POOL_WINDOWS):
        sl = slice(g * POOL_GROUP_DIM, (g + 1) * POOL_GROUP_DIM)
        x = x_ref[0, :, sl]
        half = win // 2
        lead = x
        trail = down(x, 1)
        w = 1
        while w < half:
            lead = lead + up(lead, w)
            trail = trail + down(trail, w)
            w *= 2
        lo = jnp.clip(t - half, 0, n)
        hi = jnp.clip(t - half + win, 0, n)
        pooled = (lead + trail) / (hi - lo).astype(F32) - x
        y = _dot(pooled.astype(BF16), w_ref[0, g].astype(BF16))
        o_ref[0, :, sl] = (y * sc_ref[0, :, sl]).astype(o_ref.dtype)


def _pool_mixer(rest3, base, pool_w, pool_scale, layer):
    b, n = rest3.shape[0], rest3.shape[1]
    ng = len(POOL_WINDOWS)
    return pl.pallas_call(
        functools.partial(_pool_kernel, n=n),
        grid=(b,),
        in_specs=[
            pl.BlockSpec((1, n, WIDTH_P), lambda bi: (bi, 0, (R_POOL - base) // WIDTH_P)),
            pl.BlockSpec((1, ng, POOL_GROUP_DIM, POOL_GROUP_DIM), lambda bi: (layer, 0, 0, 0)),
            pl.BlockSpec((1, 1, WIDTH_P), lambda bi: (layer, 0, 0)),
        ],
        out_specs=pl.BlockSpec((1, n, WIDTH_P), lambda bi: (bi, 0, 0)),
        out_shape=jax.ShapeDtypeStruct((b, n, WIDTH_P), BF16),
        compiler_params=_cp("arbitrary"),
        name="pool_mixer",
    )(rest3, pool_w, pool_scale)


def _ffn_up_kernel(h_ref, wg_ref, wv_ref, cw_ref, cb_ref, o_ref, *, seq):
    wg = wg_ref[0]
    wv = wv_ref[0]
    chunks = _row_chunks(h_ref.shape[0])
    g = jnp.concatenate([_dot(h_ref[r:r + m, :], wg) for r, m in chunks], axis=0)
    v = jnp.concatenate([_dot(h_ref[r:r + m, :], wv) for r, m in chunks], axis=0)
    tm = g.shape[0]
    pos = lax.broadcasted_iota(jnp.int32, g.shape, 0) % seq
    g_prev = jnp.where(pos == 0, 0.0, pltpu.roll(g, 1, 0))
    g_next = jnp.where(pos == seq - 1, 0.0, pltpu.roll(g, tm - 1, 0))
    gc = g_prev * cw_ref[0, 0:1, :] + g * cw_ref[0, 1:2, :] + g_next * cw_ref[0, 2:3, :] + cb_ref[0]
    o_ref[...] = (_silu(gc) * v).astype(o_ref.dtype)


def _ffn_up(h, w_up, layer, conv_w, conv_b, seq):
    t = h.shape[0]
    tm, tn = TM_FFN, TN_FFN
    nj = D_FF // tn
    return pl.pallas_call(
        functools.partial(_ffn_up_kernel, seq=seq),
        grid=(t // tm, nj),
        in_specs=[
            pl.BlockSpec((tm, D_MODEL), lambda i, j: (i, 0)),
            pl.BlockSpec((1, D_MODEL, tn), lambda i, j: (layer, 0, j)),
            pl.BlockSpec((1, D_MODEL, tn), lambda i, j: (layer, 0, j + nj)),
            pl.BlockSpec((1, 3, tn), lambda i, j: (layer, 0, j)),
            pl.BlockSpec((1, 1, tn), lambda i, j: (layer, 0, j)),
        ],
        out_specs=pl.BlockSpec((tm, tn), lambda i, j: (i, j)),
        out_shape=jax.ShapeDtypeStruct((t, D_FF), BF16),
        compiler_params=_cp("arbitrary", "arbitrary"),
        name="ffn_up_convglu",
    )(h, w_up, w_up, conv_w, conv_b)


def _trunk_layer(x, layer, b, n, mod, mod_row, rope, ctx, wts):
    (norm1_g, w_in, attn_lambda, subln_g, log_decay, pool_w, pool_scale, w_out, norm2_g, w_up, conv_w,
     conv_b, w_down) = wts
    h = _norm_mod(x, norm1_g, layer, mod, mod_row, SHIFT1)
    if rope is None:
        q3, k3, v3 = (_in_projection(h, w_in, layer, c0, WIDTH_A, TN_WIDE).reshape(b, n, WIDTH_A)
                      for c0 in (C_QA, C_KA, C_VA))
        rest_base = pool_base = R_QR
        rest3 = pool3 = _in_projection(h, w_in, layer, rest_base, IN_WIDTH - rest_base,
                                       TN_PROJ).reshape(b, n, -1)
        heads, state_ret = N_HEADS_A, None
        k_pieces = [(k3, (1, n, WIDTH_A), lambda bi, h: (bi, 0, 0))]
        v_pieces = [(v3, (1, n, WIDTH_A), lambda bi, h: (bi, 0, 0))]
    else:
        cache_k, cache_v, state_ret = ctx
        past = cache_k.shape[2]
        heads = ATT_HEADS_LATENT
        wd = heads * LANES
        q3 = rest3 = _in_projection(h, w_in, layer, 0, R_POOL, TN_WIDE).reshape(b, n, R_POOL)
        pool3 = _in_projection(h, w_in, layer, R_POOL, WIDTH_P, TN_PROJ).reshape(b, n, WIDTH_P)
        rest_base, pool_base = 0, R_POOL
        k3 = v3 = None
        k_pieces = [(cache_k, (1, 1, past, wd), lambda bi, h: (bi, layer, 0, h)),
                    (q3, (1, n, wd), lambda bi, h: (bi, 0, C_KA // wd + h))]
        v_pieces = [(cache_v, (1, 1, past, wd), lambda bi, h: (bi, layer, 0, h)),
                    (q3, (1, n, wd), lambda bi, h: (bi, 0, C_VA // wd + h))]
    oa = _diff_attention(q3, k_pieces, v_pieces, rope, attn_lambda, subln_g, layer, heads, min(ATT_SUB, n))
    orr, s_fin = _retention(rest3, rest_base, log_decay, layer, state_ret)
    op = _pool_mixer(pool3, pool_base, pool_w, pool_scale, layer)
    parts = [oa.reshape(b * n, WIDTH_A), orr.reshape(b * n, WIDTH_R), op.reshape(b * n, WIDTH_P)]
    x = _residual_projection_streamed(parts, w_out, layer, x, mod, mod_row, GATE1, TM_PROJ, TN_PROJ)
    h2 = _norm_mod(x, norm2_g, layer, mod, mod_row, SHIFT2)
    a = _ffn_up(h2, w_up, layer, conv_w, conv_b, n)
    x = _residual_projection([a], w_down, layer, x, mod, mod_row, GATE2, TM_DOWN, TN_DOWN)
    return x, k3, v3, s_fin


def kernel(x_prompt, x_sample, cache_k, cache_v, state_ret, c, c_ctx, ada_w, ada_b, norm1_g, w_in,
           attn_lambda, attn_subln_g, ret_log_decay, pool_w, pool_scale, w_out, norm2_g,
           ffn_w_up, ffn_conv_w, ffn_conv_b, ffn_w_down, final_g):
    bp, sp, _ = x_prompt.shape
    bs, ss, _ = x_sample.shape
    past = cache_k.shape[2]
    ctx_row = bs
    assert bs < MOD_ROWS and x_prompt.shape[2] == D_MODEL and x_sample.shape[2] == D_MODEL
    for seq, tokens in ((sp, bp * sp), (ss, bs * ss)):
        assert TM_FFN % seq == 0 and tokens % TM_FFN == 0 and tokens % TM_PROJ == 0
        assert seq % RET_CHUNK == 0 and seq % min(ATT_SUB, seq) == 0 and seq % GRID_W == 0
    assert ss % TM_PROJ == 0 and ss % TM_DOWN == 0 and ss % TM_NORM == 0

    cs = jnp.concatenate([c, c_ctx[None, :], jnp.zeros((MOD_ROWS - bs - 1, D_MODEL), F32)], axis=0)
    mod = _modulation(cs, ada_w, ada_b).reshape(DEPTH, MOD_ROWS, 6, D_MODEL)

    rope = _rope_tables(ss)
    ck = cache_k.reshape(bs, DEPTH, past, QK_A)
    cv = cache_v.reshape(bs, DEPTH, past, WIDTH_A)

    w_in_b = w_in.astype(BF16)
    w_out_b = w_out.astype(BF16)
    w_up_b = ffn_w_up.astype(BF16)
    w_down_b = ffn_w_down.astype(BF16)
    norm1_3 = norm1_g.reshape(DEPTH, 1, D_MODEL)
    norm2_3 = norm2_g.reshape(DEPTH, 1, D_MODEL)
    wts = (norm1_3, w_in_b, attn_lambda, attn_subln_g.reshape(DEPTH, 1, DV_A), ret_log_decay, pool_w,
           pool_scale.reshape(DEPTH, 1, WIDTH_P), w_out_b, norm2_3, w_up_b, ffn_conv_w,
           ffn_conv_b.reshape(DEPTH, 1, D_FF), w_down_b)

    def prompt_row(i, tm):
        return ctx_row

    def sample_row(i, tm):
        return (i * tm) // ss

    xp = x_prompt.reshape(bp * sp, D_MODEL)
    xs = x_sample.reshape(bs * ss, D_MODEL)
    new_k, new_v, new_s = [], [], []
    for l in range(DEPTH):
        xp, k_l, v_l, s_l = _trunk_layer(xp, l, bp, sp, mod, prompt_row, None, None, wts)
        new_k.append(k_l.reshape(bp, sp, 2 * N_HEADS_A, DH_A))
        new_v.append(v_l.reshape(bp, sp, N_HEADS_A, DV_A))
        new_s.append(s_l)
        xs, _, _, _ = _trunk_layer(xs, l, bs, ss, mod, sample_row, rope, (ck, cv, state_ret), wts)
    y_prompt = _final_norm(xp, final_g).reshape(bp, sp, D_MODEL)
    y_sample = _final_norm(xs, final_g).reshape(bs, ss, D_MODEL)
    return (y_prompt, y_sample, jnp.stack(new_k, axis=1), jnp.stack(new_v, axis=1), jnp.stack(new_s, axis=1))
```
